```python
import jax, jax.numpy as jnp
from jax import lax
import numpy as np

D_MODEL = 4096
BATCH = 4
SEQ = 2048
DEPTH = 2

CHUNK = 64
N_HEADS_DN = 16
HEAD_K = 128
HEAD_V = 128
D_DN_K = N_HEADS_DN * HEAD_K
D_DN_V = N_HEADS_DN * HEAD_V
DN_CONV = 4
D_SC = D_MODEL // 2
SC_CONV = 3
D_FF = 11008
EPS = 1e-6

SPLIT_SIZES = (2 * D_DN_K + D_DN_V,
               N_HEADS_DN,
               N_HEADS_DN,
               D_DN_V,
               D_SC,
               D_SC,
               D_SC,
               D_MODEL,
               D_MODEL)
D_IN = 2 * D_DN_K + D_DN_V + 2 * N_HEADS_DN + D_DN_V + 3 * D_SC + 2 * D_MODEL

kernel_name = "hybrid_gdn_shortconv_macaron_sandwich"


def _split(t, sizes):
    idx = [int(s) for s in np.cumsum(sizes)[:-1]]
    return jnp.split(t, idx, axis=-1)


def rmsnorm(x, g):
    xf = x.astype(jnp.float32)
    y = xf * lax.rsqrt(jnp.mean(xf * xf, axis=-1, keepdims=True) + EPS)
    return (y * g.astype(jnp.float32)).astype(x.dtype)


def l2norm(x):
    return x * lax.rsqrt(jnp.sum(x * x, axis=-1, keepdims=True) + EPS)


def causal_depthwise_conv(x, w):
    width = w.shape[-1]
    s = x.shape[1]
    xp = jnp.pad(x, ((0, 0), (width - 1, 0), (0, 0)))
    return sum(xp[:, j:j + s, :] * w[:, j] for j in range(width))


def swiglu_ffn(x, w_in, w_out):
    gate, up = jnp.split(x @ w_in, 2, axis=-1)
    return (jax.nn.silu(gate) * up) @ w_out


def gated_delta_rule(q, k, v, g, beta):
    bsz, s, h, _ = q.shape
    n = s // CHUNK

    def to_chunks(t):
        return t.reshape(bsz, n, CHUNK, h, -1).transpose(1, 0, 3, 2, 4)

    q, k, v = to_chunks(q), to_chunks(k), to_chunks(v)
    g = g.reshape(bsz, n, CHUNK, h).transpose(1, 0, 3, 2)
    beta = beta.reshape(bsz, n, CHUNK, h).transpose(1, 0, 3, 2)
    gc = jnp.cumsum(g, axis=-1)

    idx = jnp.arange(CHUNK)
    incl = idx[:, None] >= idx[None, :]
    strict = idx[:, None] > idx[None, :]
    diff = gc[..., :, None] - gc[..., None, :]
    decay = jnp.exp(jnp.where(incl, diff, -jnp.inf))

    k_beta = k * beta[..., None]
    v_beta = v * beta[..., None]
    kk = jnp.einsum('nbhid,nbhjd->nbhij', k_beta, k)
    lower = jnp.where(strict, kk * decay, 0.0)
    eye = jnp.eye(CHUNK, dtype=q.dtype)
    t_mat = lax.linalg.triangular_solve(eye + lower, jnp.broadcast_to(eye, lower.shape),
                                        left_side=True, lower=True, unit_diagonal=True)
    u_val = jnp.einsum('nbhij,nbhjd->nbhid', t_mat, v_beta)
    k_cum = jnp.einsum('nbhij,nbhjd->nbhid', t_mat, k_beta * jnp.exp(gc)[..., None])
    attn_intra = jnp.einsum('nbhid,nbhjd->nbhij', q, k) * decay
    q_dec = q * jnp.exp(gc)[..., None]
    g_last = gc[..., -1]
    k_dec = k * jnp.exp(g_last[..., None] - gc)[..., None]

    def step(state, xs):
        q_d, k_d, u, kc, a_in, gl = xs
        v_new = u - jnp.einsum('bhck,bhkv->bhcv', kc, state)
        o = jnp.einsum('bhck,bhkv->bhcv', q_d, state) + jnp.einsum('bhij,bhjv->bhiv', a_in, v_new)
        state = state * jnp.exp(gl)[..., None, None] + jnp.einsum('bhck,bhcv->bhkv', k_d, v_new)
        return state, o

    s0 = jnp.zeros((bsz, h, q.shape[-1], v.shape[-1]), dtype=q.dtype)
    _, o = lax.scan(step, s0, (q_dec, k_dec, u_val, k_cum, attn_intra, g_last))
    return o.transpose(1, 0, 3, 2, 4).reshape(bsz, s, h, -1)


def hybrid_mixer(u, w_in, dn_conv_w, dn_a_log, dn_dt_bias, dn_norm_g, w_dn_out,
                 sc_conv_w, w_sc_out, w_o):
    bsz, s, _ = u.shape
    qkv, a, b, z, sc_b, sc_c, sc_x, gate_a, gate_b = _split(u @ w_in, SPLIT_SIZES)

    qkv = jax.nn.silu(causal_depthwise_conv(qkv, dn_conv_w))
    q, k, v = _split(qkv, (D_DN_K, D_DN_K, D_DN_V))
    q = l2norm(q.reshape(bsz, s, N_HEADS_DN, HEAD_K).astype(jnp.float32)) * (HEAD_K ** -0.5)
    k = l2norm(k.reshape(bsz, s, N_HEADS_DN, HEAD_K).astype(jnp.float32))
    v = v.reshape(bsz, s, N_HEADS_DN, HEAD_V).astype(jnp.float32)
    g = -jnp.exp(dn_a_log.astype(jnp.float32)) * jax.nn.softplus(
        a.astype(jnp.float32) + dn_dt_bias.astype(jnp.float32))
    beta = jax.nn.sigmoid(b.astype(jnp.float32))
    o = gated_delta_rule(q, k, v, g, beta).astype(u.dtype)
    o = rmsnorm(o, dn_norm_g) * jax.nn.silu(z.reshape(bsz, s, N_HEADS_DN, HEAD_V))
    y_a = o.reshape(bsz, s, D_DN_V) @ w_dn_out

    y_b = (sc_b * causal_depthwise_conv(sc_c * sc_x, sc_conv_w)) @ w_sc_out

    merged = jax.nn.sigmoid(gate_a) * y_a + jax.nn.sigmoid(gate_b) * y_b
    return merged @ w_o


def setup_inputs(seed: int = 0) -> dict:
    key = jax.random.key(seed)
    ks = jax.random.split(key, 24)

    def nrm(k, shape, fan_in):
        return jax.random.normal(k, shape, jnp.float32) * (fan_in ** -0.5)

    def gain(k, shape):
        return 1.0 + 0.05 * jax.random.normal(k, shape, jnp.float32)

    L = DEPTH
    dt = jnp.exp(jax.random.uniform(ks[10], (L, N_HEADS_DN), jnp.float32,
                                    np.log(1e-3), np.log(1e-1)))
    return {
        "x": jax.random.normal(ks[0], (BATCH, SEQ, D_MODEL), jnp.float32),
        "ffn1_pre_g": gain(ks[1], (L, D_MODEL)),
        "ffn1_post_g": gain(ks[2], (L, D_MODEL)),
        "w_ffn1_in": nrm(ks[3], (L, D_MODEL, 2 * D_FF), D_MODEL),
        "w_ffn1_out": nrm(ks[4], (L, D_FF, D_MODEL), D_FF),
        "mix_pre_g": gain(ks[5], (L, D_MODEL)),
        "mix_post_g": gain(ks[6], (L, D_MODEL)),
        "w_in": nrm(ks[7], (L, D_MODEL, D_IN), D_MODEL),
        "dn_conv_w": nrm(ks[8], (L, 2 * D_DN_K + D_DN_V, DN_CONV), DN_CONV),
        "dn_a_log": jnp.log(jax.random.uniform(ks[9], (L, N_HEADS_DN), jnp.float32, 1.0, 16.0)),
        "dn_dt_bias": jnp.log(jnp.expm1(dt)),
        "dn_norm_g": gain(ks[11], (L, HEAD_V)),
        "w_dn_out": nrm(ks[12], (L, D_DN_V, D_MODEL), D_DN_V),
        "sc_conv_w": nrm(ks[13], (L, D_SC, SC_CONV), SC_CONV),
        "w_sc_out": nrm(ks[14], (L, D_SC, D_MODEL), D_SC),
        "w_o": nrm(ks[15], (L, D_MODEL, D_MODEL), D_MODEL),
        "ffn2_pre_g": gain(ks[16], (L, D_MODEL)),
        "ffn2_post_g": gain(ks[17], (L, D_MODEL)),
        "w_ffn2_in": nrm(ks[18], (L, D_MODEL, 2 * D_FF), D_MODEL),
        "w_ffn2_out": nrm(ks[19], (L, D_FF, D_MODEL), D_FF),
    }


def reference(x, ffn1_pre_g, ffn1_post_g, w_ffn1_in, w_ffn1_out, mix_pre_g, mix_post_g,
              w_in, dn_conv_w, dn_a_log, dn_dt_bias, dn_norm_g, w_dn_out, sc_conv_w,
              w_sc_out, w_o, ffn2_pre_g, ffn2_post_g, w_ffn2_in, w_ffn2_out):
    h = x
    for l in range(DEPTH):
        f1 = swiglu_ffn(rmsnorm(h, ffn1_pre_g[l]), w_ffn1_in[l], w_ffn1_out[l])
        h = h + 0.5 * rmsnorm(f1, ffn1_post_g[l])
        m = hybrid_mixer(rmsnorm(h, mix_pre_g[l]), w_in[l], dn_conv_w[l], dn_a_log[l],
                         dn_dt_bias[l], dn_norm_g[l], w_dn_out[l], sc_conv_w[l],
                         w_sc_out[l], w_o[l])
        h = h + rmsnorm(m, mix_post_g[l])
        f2 = swiglu_ffn(rmsnorm(h, ffn2_pre_g[l]), w_ffn2_in[l], w_ffn2_out[l])
        h = h + 0.5 * rmsnorm(f2, ffn2_post_g[l])
    return h
```

```python
import functools

import jax
import jax.numpy as jnp
from jax import lax
from jax.experimental import pallas as pl
from jax.experimental.pallas import tpu as pltpu

D_MODEL = 4096
BATCH = 4
SEQ = 2048
DEPTH = 2
CHUNK = 64
N_HEADS = 16
HEAD_D = 128
D_DN = N_HEADS * HEAD_D
D_QKV = 3 * D_DN
DN_CONV = 4
D_SC = D_MODEL // 2
SC_CONV = 3
D_FF = 11008
EPS = 1e-6
M_TOK = BATCH * SEQ

OFF_Z = D_QKV
OFF_SCB = OFF_Z + D_DN
OFF_SCC = OFF_SCB + D_SC
OFF_SCX = OFF_SCC + D_SC
OFF_GA = OFF_SCX + D_SC
OFF_GB = OFF_GA + D_MODEL
OFF_AB = OFF_GB + D_MODEL
AB_PAD = 512
D_PROJ = OFF_AB + AB_PAD

SUPER = 2 * CHUNK
N_SUPER = SEQ // SUPER
N_CHUNK = SEQ // CHUNK
SB_PER_ITER = 4

VMEM_LIMIT = 60 * 1024 * 1024

F32 = jnp.float32
BF16 = jnp.bfloat16


def _cparams(n_axes):
    return pltpu.CompilerParams(
        dimension_semantics=("arbitrary",) * n_axes, vmem_limit_bytes=VMEM_LIMIT)


def _bdot(a, b):
    return jnp.dot(a.astype(BF16), b.astype(BF16), preferred_element_type=F32)


def _bdot_nt(a, b):
    return lax.dot_general(a.astype(BF16), b.astype(BF16), (((1,), (1,)), ((), ())),
                           preferred_element_type=F32)


def _bdot_tn(a, b):
    return lax.dot_general(a.astype(BF16), b.astype(BF16), (((0,), (0,)), ((), ())),
                           preferred_element_type=F32)


def _rms_scale(x):
    return lax.rsqrt(jnp.mean(x * x, axis=-1, keepdims=True) + EPS)


def _rmsnorm_cast_kernel(x_ref, g_ref, o_ref):
    x = x_ref[...]
    o_ref[...] = ((x * _rms_scale(x)) * g_ref[...]).astype(o_ref.dtype)


def rmsnorm_cast(x, g, *, tr=256):
    m, d = x.shape
    return pl.pallas_call(
        _rmsnorm_cast_kernel,
        grid=(m // tr,),
        in_specs=[pl.BlockSpec((tr, d), lambda i: (i, 0)),
                  pl.BlockSpec((1, d), lambda i: (0, 0))],
        out_specs=pl.BlockSpec((tr, d), lambda i: (i, 0)),
        out_shape=jax.ShapeDtypeStruct((m, d), BF16),
        compiler_params=_cparams(1),
        name="rmsnorm_cast",
    )(x, g.reshape(1, d))


def _resid_norm_kernel(h_ref, f_ref, gpost_ref, gpre_ref, hout_ref, xn_ref, *, scale):
    f = f_ref[...]
    h = h_ref[...] + scale * ((f * _rms_scale(f)) * gpost_ref[...])
    hout_ref[...] = h
    xn_ref[...] = ((h * _rms_scale(h)) * gpre_ref[...]).astype(xn_ref.dtype)


def _resid_kernel(h_ref, f_ref, gpost_ref, hout_ref, *, scale):
    f = f_ref[...]
    hout_ref[...] = h_ref[...] + scale * ((f * _rms_scale(f)) * gpost_ref[...])


def resid_norm(h, f, gpost, gpre, scale, *, tr=256):
    m, d = h.shape
    row = pl.BlockSpec((tr, d), lambda i: (i, 0))
    vec = pl.BlockSpec((1, d), lambda i: (0, 0))
    if gpre is None:
        return pl.pallas_call(
            functools.partial(_resid_kernel, scale=scale),
            grid=(m // tr,),
            in_specs=[row, row, vec],
            out_specs=row,
            out_shape=jax.ShapeDtypeStruct((m, d), F32),
            compiler_params=_cparams(1),
            name="resid_final",
        )(h, f, gpost.reshape(1, d)), None
    return pl.pallas_call(
        functools.partial(_resid_norm_kernel, scale=scale),
        grid=(m // tr,),
        in_specs=[row, row, vec, vec],
        out_specs=[row, row],
        out_shape=[jax.ShapeDtypeStruct((m, d), F32), jax.ShapeDtypeStruct((m, d), BF16)],
        compiler_params=_cparams(1),
        name="resid_norm",
    )(h, f, gpost.reshape(1, d), gpre.reshape(1, d))


def _matmul_kernel(a_ref, b_ref, o_ref):
    o_ref[...] = jnp.dot(a_ref[...], b_ref[...], preferred_element_type=F32).astype(o_ref.dtype)


def matmul(a, b, *, tm, tn, out_dtype=F32, name="matmul"):
    m, k = a.shape
    _, n = b.shape
    return pl.pallas_call(
        _matmul_kernel,
        grid=(m // tm, n // tn),
        in_specs=[pl.BlockSpec((tm, k), lambda i, j: (i, 0)),
                  pl.BlockSpec((k, tn), lambda i, j: (0, j))],
        out_specs=pl.BlockSpec((tm, tn), lambda i, j: (i, j)),
        out_shape=jax.ShapeDtypeStruct((m, n), out_dtype),
        compiler_params=_cparams(2),
        name=name,
    )(a, b)


def _ffn_in_kernel(x_ref, wg_ref, wu_ref, o_ref):
    x = x_ref[...]
    gate = jnp.dot(x, wg_ref[...], preferred_element_type=F32)
    up = jnp.dot(x, wu_ref[...], preferred_element_type=F32)
    o_ref[...] = ((gate * jax.nn.sigmoid(gate)) * up).astype(o_ref.dtype)


def ffn_in(xn, wg, wu, *, tm=1024, tn=256):
    m, k = xn.shape
    n = wg.shape[1]
    return pl.pallas_call(
        _ffn_in_kernel,
        grid=(m // tm, n // tn),
        in_specs=[pl.BlockSpec((tm, k), lambda i, j: (i, 0)),
                  pl.BlockSpec((k, tn), lambda i, j: (0, j)),
                  pl.BlockSpec((k, tn), lambda i, j: (0, j))],
        out_specs=pl.BlockSpec((tm, tn), lambda i, j: (i, j)),
        out_shape=jax.ShapeDtypeStruct((m, n), BF16),
        compiler_params=_cparams(2),
        name="ffn_in",
    )(xn, wg, wu)


def _merge_kernel(oa_ref, ob_ref, wa_ref, wb_ref, ga_ref, gb_ref, o_ref):
    ya = jnp.dot(oa_ref[...], wa_ref[...], preferred_element_type=F32)
    yb = jnp.dot(ob_ref[...], wb_ref[...], preferred_element_type=F32)
    merged = jax.nn.sigmoid(ga_ref[...]) * ya + jax.nn.sigmoid(gb_ref[...]) * yb
    o_ref[...] = merged.astype(o_ref.dtype)


def merge_branches(oa, ob, wa, wb, proj, *, tm=1024, tn=512):
    m, k = oa.shape
    n = wa.shape[1]
    ga0, gb0 = OFF_GA // tn, OFF_GB // tn
    return pl.pallas_call(
        _merge_kernel,
        grid=(m // tm, n // tn),
        in_specs=[pl.BlockSpec((tm, k), lambda i, j: (i, 0)),
                  pl.BlockSpec((tm, k), lambda i, j: (i, 0)),
                  pl.BlockSpec((k, tn), lambda i, j: (0, j)),
                  pl.BlockSpec((k, tn), lambda i, j: (0, j)),
                  pl.BlockSpec((tm, tn), lambda i, j: (i, ga0 + j)),
                  pl.BlockSpec((tm, tn), lambda i, j: (i, gb0 + j))],
        out_specs=pl.BlockSpec((tm, tn), lambda i, j: (i, j)),
        out_shape=jax.ShapeDtypeStruct((m, n), BF16),
        compiler_params=_cparams(2),
        name="merge_branches",
    )(oa, ob, wa, wb, proj, proj)


def _shift_rows(x, s, row):
    return jnp.where(row >= s, pltpu.roll(x, s, axis=0), 0.0)


def _short_conv_kernel(b_ref, c_ref, x_ref, w_ref, o_ref):
    cx = c_ref[...] * x_ref[...]
    row = lax.broadcasted_iota(jnp.int32, cx.shape, 0)
    w = w_ref[...]
    acc = _shift_rows(cx, SC_CONV - 1, row) * w[0:1, :]
    for j in range(1, SC_CONV):
        s = SC_CONV - 1 - j
        term = cx if s == 0 else _shift_rows(cx, s, row)
        acc = acc + term * w[j:j + 1, :]
    o_ref[...] = (b_ref[...] * acc).astype(o_ref.dtype)


def short_conv(proj, w_t, *, tc=256):
    nb = D_SC // tc
    return pl.pallas_call(
        _short_conv_kernel,
        grid=(BATCH, nb),
        in_specs=[pl.BlockSpec((SEQ, tc), lambda b, j: (b, OFF_SCB // tc + j)),
                  pl.BlockSpec((SEQ, tc), lambda b, j: (b, OFF_SCC // tc + j)),
                  pl.BlockSpec((SEQ, tc), lambda b, j: (b, OFF_SCX // tc + j)),
                  pl.BlockSpec((SC_CONV, tc), lambda b, j: (0, j))],
        out_specs=pl.BlockSpec((SEQ, tc), lambda b, j: (b, j)),
        out_shape=jax.ShapeDtypeStruct((M_TOK, D_SC), BF16),
        compiler_params=_cparams(2),
        name="short_conv",
    )(proj, proj, proj, w_t)


def _conv4_silu(x, w, row):
    acc = _shift_rows(x, DN_CONV - 1, row) * w[0:1, :]
    for j in range(1, DN_CONV):
        s = DN_CONV - 1 - j
        term = x if s == 0 else _shift_rows(x, s, row)
        acc = acc + term * w[j:j + 1, :]
    return acc * jax.nn.sigmoid(acc)


def _l2norm(x):
    return x * lax.rsqrt(jnp.sum(x * x, axis=-1, keepdims=True) + EPS)


def _pick_lane(x, lane, idx):
    picked = jnp.sum(jnp.where(lane == idx, x, 0.0), axis=-1, keepdims=True)
    return jnp.broadcast_to(picked, x.shape)


def _deltanet_kernel(xq_ref, xk_ref, xv_ref, z_ref, ab_ref, wq_ref, wk_ref, wv_ref,
                     alog_ref, dtb_ref, ng_ref, o_ref,
                     q_s, k_s, v_s, gc_s, beta_s, egl_s, kq_s, u_s, kd_s, attn_s,
                     o1_s, vnew_s):
    h = pl.program_id(1)
    row = lax.broadcasted_iota(jnp.int32, (SEQ, HEAD_D), 0)
    lane = lax.broadcasted_iota(jnp.int32, (SEQ, HEAD_D), 1)

    q_s[...] = _l2norm(_conv4_silu(xq_ref[...], wq_ref[...], row)) * (HEAD_D ** -0.5)
    k_s[...] = _l2norm(_conv4_silu(xk_ref[...], wk_ref[...], row))
    v_s[...] = _conv4_silu(xv_ref[...], wv_ref[...], row)

    ab = ab_ref[...]
    a_h = _pick_lane(ab, lane, h)
    b_h = _pick_lane(ab, lane, N_HEADS + h)
    lane1 = lax.broadcasted_iota(jnp.int32, (1, HEAD_D), 1)
    alog_h = jnp.sum(jnp.where(lane1 == h, alog_ref[...], 0.0), axis=-1, keepdims=True)
    dtb_h = jnp.sum(jnp.where(lane1 == h, dtb_ref[...], 0.0), axis=-1, keepdims=True)
    g = -jnp.exp(alog_h) * jax.nn.softplus(a_h + dtb_h)
    beta_s[...] = jax.nn.sigmoid(b_h)

    pos = row % CHUNK
    gc = g
    step = 1
    while step < CHUNK:
        gc = gc + jnp.where(pos >= step, pltpu.roll(gc, step, axis=0), 0.0)
        step *= 2
    gc_s[...] = gc

    r_i = lax.broadcasted_iota(jnp.int32, (SUPER, SUPER), 0)
    c_i = lax.broadcasted_iota(jnp.int32, (SUPER, SUPER), 1)
    same = (r_i // CHUNK) == (c_i // CHUNK)
    incl = same & (r_i >= c_i)
    strict = same & (r_i > c_i)
    eye = (r_i == c_i).astype(F32)
    first_half = r_i < CHUNK

    def phase1(it, carry):
        for s in range(SB_PER_ITER):
            sb = it * SB_PER_ITER + s
            r0 = pl.multiple_of(sb * SUPER, SUPER)
            rows = pl.ds(r0, SUPER)
            q = q_s[rows, :]
            k = k_s[rows, :]
            v = v_s[rows, :]
            cb = gc_s[rows, :]
            beta = beta_s[rows, :]
            rb = cb.T
            decay = jnp.exp(jnp.where(incl, cb - rb, -jnp.inf))
            kb = k * beta
            vb = v * beta
            lower = jnp.where(strict, _bdot_nt(kb, k) * decay, 0.0)
            p = -lower
            t = eye + p
            n_sq = 1
            while 2 * n_sq < CHUNK:
                p = _bdot(p, p)
                t = t + _bdot(t, p)
                n_sq *= 2
            egc = jnp.exp(cb)
            u_s[rows, :] = _bdot(t, vb)
            kcum = _bdot(t, kb * egc)
            attn_s[rows, :] = _bdot_nt(q, k) * decay
            qd = q * egc
            gl = jnp.where(first_half,
                           jnp.broadcast_to(cb[CHUNK - 1:CHUNK, :], (SUPER, SUPER)),
                           jnp.broadcast_to(cb[SUPER - 1:SUPER, :], (SUPER, SUPER)))
            kd_s[rows, :] = k * jnp.exp(gl - cb)
            egl_s[rows, :] = jnp.exp(gl)
            kq0 = pl.multiple_of(sb * 2 * SUPER, SUPER)
            kq_s[pl.ds(kq0, CHUNK), :] = kcum[0:CHUNK]
            kq_s[pl.ds(kq0 + CHUNK, CHUNK), :] = qd[0:CHUNK]
            kq_s[pl.ds(kq0 + SUPER, CHUNK), :] = kcum[CHUNK:SUPER]
            kq_s[pl.ds(kq0 + SUPER + CHUNK, CHUNK), :] = qd[CHUNK:SUPER]
        return carry

    lax.fori_loop(0, N_SUPER // SB_PER_ITER, phase1, 0)

    def phase2(c, state):
        r0 = pl.multiple_of(c * CHUNK, CHUNK)
        rows = pl.ds(r0, CHUNK)
        kq = kq_s[pl.ds(pl.multiple_of(c * SUPER, SUPER), SUPER), :]
        ks = _bdot(kq, state)
        vnew = u_s[rows, :] - ks[0:CHUNK]
        o1_s[rows, :] = ks[CHUNK:SUPER]
        vnew_s[rows, :] = vnew
        egl = egl_s[pl.ds(r0, 8), :]
        scaled = (state.reshape(HEAD_D // 8, 8, HEAD_D) * egl[None]).reshape(HEAD_D, HEAD_D)
        return scaled + _bdot_tn(kd_s[rows, :], vnew)

    lax.fori_loop(0, N_CHUNK, phase2, jnp.zeros((HEAD_D, HEAD_D), F32))

    ng = ng_ref[...]

    def phase3(it, carry):
        for s in range(SB_PER_ITER):
            sb = it * SB_PER_ITER + s
            rows = pl.ds(pl.multiple_of(sb * SUPER, SUPER), SUPER)
            o = o1_s[rows, :] + _bdot(attn_s[rows, :], vnew_s[rows, :])
            z = z_ref[rows, :]
            on = (o * _rms_scale(o)) * ng
            o_ref[rows, :] = (on * (z * jax.nn.sigmoid(z))).astype(o_ref.dtype)
        return carry

    lax.fori_loop(0, N_SUPER // SB_PER_ITER, phase3, 0)


def deltanet(proj, conv_w_t, a_log, dt_bias, norm_g):
    nh = N_HEADS

    def col(off):
        return lambda b, h: (b, off // HEAD_D + h)

    def wcol(off):
        return lambda b, h: (0, off // HEAD_D + h)

    seq_blk = lambda off: pl.BlockSpec((SEQ, HEAD_D), col(off))
    w_blk = lambda off: pl.BlockSpec((DN_CONV, HEAD_D), wcol(off))
    vec = pl.BlockSpec((1, HEAD_D), lambda b, h: (0, 0))
    pad = lambda x: jnp.pad(x, (0, HEAD_D - nh)).reshape(1, HEAD_D)
    big = pltpu.VMEM((SEQ, HEAD_D), F32)
    return pl.pallas_call(
        _deltanet_kernel,
        grid=(BATCH, nh),
        in_specs=[seq_blk(0), seq_blk(D_DN), seq_blk(2 * D_DN), seq_blk(OFF_Z),
                  pl.BlockSpec((SEQ, HEAD_D), lambda b, h: (b, OFF_AB // HEAD_D)),
                  w_blk(0), w_blk(D_DN), w_blk(2 * D_DN), vec, vec, vec],
        out_specs=pl.BlockSpec((SEQ, HEAD_D), lambda b, h: (b, h)),
        out_shape=jax.ShapeDtypeStruct((M_TOK, D_DN), BF16),
        scratch_shapes=[big, big, big, big, big, big,
                        pltpu.VMEM((2 * SEQ, HEAD_D), F32),
                        big, big, big, big, big],
        compiler_params=_cparams(2),
        name="deltanet",
    )(proj, proj, proj, proj, proj, conv_w_t, conv_w_t, conv_w_t,
      pad(a_log), pad(dt_bias), norm_g.reshape(1, HEAD_D))


def _ffn_block(h, xn, w_in, w_out, gpost, gpre_next):
    wg = w_in[:, :D_FF].astype(BF16)
    wu = w_in[:, D_FF:].astype(BF16)
    act = ffn_in(xn, wg, wu)
    f = matmul(act, w_out.astype(BF16), tm=512, tn=256, name="ffn_out")
    return resid_norm(h, f, gpost, gpre_next, 0.5)


def _mixer_block(h, xn, w_in, dn_conv_w, dn_a_log, dn_dt_bias, dn_norm_g, w_dn_out,
                 sc_conv_w, w_sc_out, w_o, gpost, gpre_next):
    ab0 = D_QKV
    w_cat = jnp.concatenate(
        [w_in[:, :D_QKV], w_in[:, ab0 + 2 * N_HEADS:], w_in[:, ab0:ab0 + 2 * N_HEADS],
         jnp.zeros((D_MODEL, AB_PAD - 2 * N_HEADS), w_in.dtype)], axis=1).astype(BF16)
    proj = matmul(xn, w_cat, tm=1024, tn=512, name="in_proj")
    o_dn = deltanet(proj, dn_conv_w.T, dn_a_log, dn_dt_bias, dn_norm_g)
    o_sc = short_conv(proj, sc_conv_w.T)
    merged = merge_branches(o_dn, o_sc, w_dn_out.astype(BF16), w_sc_out.astype(BF16), proj)
    m = matmul(merged, w_o.astype(BF16), tm=1024, tn=512, name="out_proj")
    return resid_norm(h, m, gpost, gpre_next, 1.0)


def kernel(x, ffn1_pre_g, ffn1_post_g, w_ffn1_in, w_ffn1_out, mix_pre_g, mix_post_g, w_in,
           dn_conv_w, dn_a_log, dn_dt_bias, dn_norm_g, w_dn_out, sc_conv_w, w_sc_out, w_o,
           ffn2_pre_g, ffn2_post_g, w_ffn2_in, w_ffn2_out):
    h = x.reshape(M_TOK, D_MODEL)
    xn = rmsnorm_cast(h, ffn1_pre_g[0])
    for l in range(DEPTH):
        h, xn = _ffn_block(h, xn, w_ffn1_in[l], w_ffn1_out[l], ffn1_post_g[l], mix_pre_g[l])
        h, xn = _mixer_block(h, xn, w_in[l], dn_conv_w[l], dn_a_log[l], dn_dt_bias[l],
                             dn_norm_g[l], w_dn_out[l], sc_conv_w[l], w_sc_out[l], w_o[l],
                             mix_post_g[l], ffn2_pre_g[l])
        nxt = ffn1_pre_g[l + 1] if l + 1 < DEPTH else None
        h, xn = _ffn_block(h, xn, w_ffn2_in[l], w_ffn2_out[l], ffn2_post_g[l], nxt)
    return h.reshape(BATCH, SEQ, D_MODEL)
```

```python
import functools

import jax
import jax.numpy as jnp
from jax import lax
from jax.experimental import pallas as pl
from jax.experimental.pallas import tpu as pltpu

D_MODEL = 4096
BATCH = 4
SEQ = 2048
DEPTH = 2
CHUNK = 64
N_HEADS = 16
HEAD_D = 128
D_DN = N_HEADS * HEAD_D
D_QKV = 3 * D_DN
DN_CONV = 4
D_SC = D_MODEL // 2
SC_CONV = 3
D_FF = 11008
EPS = 1e-6
M_TOK = BATCH * SEQ

OFF_Z = 0
OFF_SCB = OFF_Z + D_DN
OFF_SCC = OFF_SCB + D_SC
OFF_SCX = OFF_SCC + D_SC
OFF_GA = OFF_SCX + D_SC
OFF_GB = OFF_GA + D_MODEL
D_REST = OFF_GB + D_MODEL

SUPER = 2 * CHUNK
N_SUPER = SEQ // SUPER
N_CHUNK = SEQ // CHUNK
SB_PER_ITER = 4
HEADS_PER_STEP = 2
SUBLANES = 8

VMEM_LIMIT = 60 * 1024 * 1024

F32 = jnp.float32
BF16 = jnp.bfloat16


def _cparams(n_axes):
    return pltpu.CompilerParams(
        dimension_semantics=("arbitrary",) * n_axes, vmem_limit_bytes=VMEM_LIMIT)


def _bdot(a, b):
    return jnp.dot(a.astype(BF16), b.astype(BF16), preferred_element_type=F32)


def _bdot_nt(a, b):
    return lax.dot_general(a.astype(BF16), b.astype(BF16), (((1,), (1,)), ((), ())),
                           preferred_element_type=F32)


def _bdot_tn(a, b):
    return lax.dot_general(a.astype(BF16), b.astype(BF16), (((0,), (0,)), ((), ())),
                           preferred_element_type=F32)


def _rms_scale(x):
    return lax.rsqrt(jnp.mean(x * x, axis=-1, keepdims=True) + EPS)


def _rmsnorm_cast_kernel(x_ref, g_ref, o_ref):
    x = x_ref[...]
    o_ref[...] = ((x * _rms_scale(x)) * g_ref[...]).astype(o_ref.dtype)


def rmsnorm_cast(x, g, *, tr=256):
    m, d = x.shape
    return pl.pallas_call(
        _rmsnorm_cast_kernel,
        grid=(m // tr,),
        in_specs=[pl.BlockSpec((tr, d), lambda i: (i, 0)),
                  pl.BlockSpec((1, d), lambda i: (0, 0))],
        out_specs=pl.BlockSpec((tr, d), lambda i: (i, 0)),
        out_shape=jax.ShapeDtypeStruct((m, d), BF16),
        compiler_params=_cparams(1),
        name="rmsnorm_cast",
    )(x, g.reshape(1, d))


def _resid_norm_kernel(h_ref, f_ref, gpost_ref, gpre_ref, hout_ref, xn_ref, *, scale):
    f = f_ref[...]
    h = h_ref[...] + scale * ((f * _rms_scale(f)) * gpost_ref[...])
    hout_ref[...] = h
    xn_ref[...] = ((h * _rms_scale(h)) * gpre_ref[...]).astype(xn_ref.dtype)


def _resid_kernel(h_ref, f_ref, gpost_ref, hout_ref, *, scale):
    f = f_ref[...]
    hout_ref[...] = h_ref[...] + scale * ((f * _rms_scale(f)) * gpost_ref[...])


def resid_norm(h, f, gpost, gpre, scale, *, tr=256):
    m, d = h.shape
    row = pl.BlockSpec((tr, d), lambda i: (i, 0))
    vec = pl.BlockSpec((1, d), lambda i: (0, 0))
    if gpre is None:
        return pl.pallas_call(
            functools.partial(_resid_kernel, scale=scale),
            grid=(m // tr,),
            in_specs=[row, row, vec],
            out_specs=row,
            out_shape=jax.ShapeDtypeStruct((m, d), F32),
            compiler_params=_cparams(1),
            name="resid_final",
        )(h, f, gpost.reshape(1, d)), None
    return pl.pallas_call(
        functools.partial(_resid_norm_kernel, scale=scale),
        grid=(m // tr,),
        in_specs=[row, row, vec, vec],
        out_specs=[row, row],
        out_shape=[jax.ShapeDtypeStruct((m, d), F32), jax.ShapeDtypeStruct((m, d), BF16)],
        compiler_params=_cparams(1),
        name="resid_norm",
    )(h, f, gpost.reshape(1, d), gpre.reshape(1, d))


def _resident(shape, index_map):
    return pl.BlockSpec(shape, index_map, pipeline_mode=pl.Buffered(1))


def _w_spec(w, k, tn, layer, col_of):
    if w.ndim == 2:
        return pl.BlockSpec((k, tn), lambda i, j: (0, col_of(j)))
    return pl.BlockSpec((None, k, tn), lambda i, j: (layer, 0, col_of(j)))


def _matmul_kernel(a_ref, b_ref, o_ref, *scratch):
    if scratch:
        (bb_ref,) = scratch
        bb_ref[...] = b_ref[...].astype(BF16)
        b = bb_ref[...]
    else:
        b = b_ref[...]
    o_ref[...] = jnp.dot(a_ref[...], b, preferred_element_type=F32).astype(o_ref.dtype)


def matmul(a, w, *, tm, tn, layer=None, col0=0, n=None, out_dtype=F32, name="matmul"):
    m, k = a.shape
    n = w.shape[-1] - col0 if n is None else n
    j0 = col0 // tn
    scratch = [pltpu.VMEM((k, tn), BF16)] if w.dtype != BF16 else []
    return pl.pallas_call(
        _matmul_kernel,
        grid=(m // tm, n // tn),
        in_specs=[_resident((tm, k), lambda i, j: (i, 0)),
                  _w_spec(w, k, tn, layer, lambda j: j0 + j)],
        out_specs=pl.BlockSpec((tm, tn), lambda i, j: (i, j)),
        out_shape=jax.ShapeDtypeStruct((m, n), out_dtype),
        scratch_shapes=scratch,
        compiler_params=_cparams(2),
        name=name,
    )(a, w)


def _ffn_in_kernel(x_ref, wg_ref, wu_ref, o_ref, wgb_ref, wub_ref):
    wgb_ref[...] = wg_ref[...].astype(BF16)
    wub_ref[...] = wu_ref[...].astype(BF16)
    x = x_ref[...]
    gate = jnp.dot(x, wgb_ref[...], preferred_element_type=F32)
    up = jnp.dot(x, wub_ref[...], preferred_element_type=F32)
    o_ref[...] = ((gate * jax.nn.sigmoid(gate)) * up).astype(o_ref.dtype)


def ffn_in(xn, w, layer, *, tm=2048, tn=256):
    m, k = xn.shape
    nj = D_FF // tn
    return pl.pallas_call(
        _ffn_in_kernel,
        grid=(m // tm, nj),
        in_specs=[_resident((tm, k), lambda i, j: (i, 0)),
                  _w_spec(w, k, tn, layer, lambda j: j),
                  _w_spec(w, k, tn, layer, lambda j: nj + j)],
        out_specs=pl.BlockSpec((tm, tn), lambda i, j: (i, j)),
        out_shape=jax.ShapeDtypeStruct((m, D_FF), BF16),
        scratch_shapes=[pltpu.VMEM((k, tn), BF16), pltpu.VMEM((k, tn), BF16)],
        compiler_params=_cparams(2),
        name="ffn_in",
    )(xn, w, w)


def _merge_kernel(oa_ref, ob_ref, wa_ref, wb_ref, ga_ref, gb_ref, o_ref, wab_ref, wbb_ref):
    wab_ref[...] = wa_ref[...].astype(BF16)
    wbb_ref[...] = wb_ref[...].astype(BF16)
    ya = jnp.dot(oa_ref[...], wab_ref[...], preferred_element_type=F32)
    yb = jnp.dot(ob_ref[...], wbb_ref[...], preferred_element_type=F32)
    merged = jax.nn.sigmoid(ga_ref[...]) * ya + jax.nn.sigmoid(gb_ref[...]) * yb
    o_ref[...] = merged.astype(o_ref.dtype)


def merge_branches(oa, ob, wa, wb, layer, rest, *, tm=1024, tn=512):
    m, k = oa.shape
    n = wa.shape[-1]
    ga0, gb0 = OFF_GA // tn, OFF_GB // tn
    return pl.pallas_call(
        _merge_kernel,
        grid=(m // tm, n // tn),
        in_specs=[_resident((tm, k), lambda i, j: (i, 0)),
                  _resident((tm, k), lambda i, j: (i, 0)),
                  _w_spec(wa, k, tn, layer, lambda j: j),
                  _w_spec(wb, k, tn, layer, lambda j: j),
                  pl.BlockSpec((tm, tn), lambda i, j: (i, ga0 + j)),
                  pl.BlockSpec((tm, tn), lambda i, j: (i, gb0 + j))],
        out_specs=pl.BlockSpec((tm, tn), lambda i, j: (i, j)),
        out_shape=jax.ShapeDtypeStruct((m, n), BF16),
        scratch_shapes=[pltpu.VMEM((k, tn), BF16), pltpu.VMEM((k, tn), BF16)],
        compiler_params=_cparams(2),
        name="merge_branches",
    )(oa, ob, wa, wb, rest, rest)


def _shift_rows(x, s):
    r = pltpu.roll(x, s, axis=0)
    row = lax.broadcasted_iota(jnp.int32, (SUBLANES, x.shape[1]), 0)
    head = jnp.where(row >= s, r[0:SUBLANES], 0.0)
    return jnp.concatenate([head, r[SUBLANES:]], axis=0)


def _causal_conv(x, w, width):
    acc = _shift_rows(x, width - 1) * w[0:1, :]
    for j in range(1, width):
        s = width - 1 - j
        term = x if s == 0 else _shift_rows(x, s)
        acc = acc + term * w[j:j + 1, :]
    return acc


def _short_conv_kernel(b_ref, c_ref, x_ref, w_ref, o_ref):
    acc = _causal_conv(c_ref[...] * x_ref[...], w_ref[...], SC_CONV)
    o_ref[...] = (b_ref[...] * acc).astype(o_ref.dtype)


def short_conv(rest, w_t, *, tc=256):
    nb = D_SC // tc
    return pl.pallas_call(
        _short_conv_kernel,
        grid=(BATCH, nb),
        in_specs=[pl.BlockSpec((SEQ, tc), lambda b, j: (b, OFF_SCB // tc + j)),
                  pl.BlockSpec((SEQ, tc), lambda b, j: (b, OFF_SCC // tc + j)),
                  pl.BlockSpec((SEQ, tc), lambda b, j: (b, OFF_SCX // tc + j)),
                  pl.BlockSpec((SC_CONV, tc), lambda b, j: (0, j))],
        out_specs=pl.BlockSpec((SEQ, tc), lambda b, j: (b, j)),
        out_shape=jax.ShapeDtypeStruct((M_TOK, D_SC), BF16),
        compiler_params=_cparams(2),
        name="short_conv",
    )(rest, rest, rest, w_t)


def _l2norm(x):
    return x * lax.rsqrt(jnp.sum(x * x, axis=-1, keepdims=True) + EPS)


def _silu(x):
    return x * jax.nn.sigmoid(x)


def _pick_lane(x, lane, idx):
    picked = jnp.sum(jnp.where(lane == idx, x, 0.0), axis=-1, keepdims=True)
    return jnp.broadcast_to(picked, x.shape)


def _deltanet_kernel(xq_ref, xk_ref, xv_ref, z_ref, ab_ref, wq_ref, wk_ref, wv_ref,
                     alog_ref, dtb_ref, ng_ref, o_ref,
                     q_s, k_s, v_s, gcall_s, betaall_s, gc_s, beta_s,
                     kq_s, u_s, attn_s, m_s, b_s, egl_s, hist_s):
    nb = SB_PER_ITER
    hps = HEADS_PER_STEP
    head0 = pl.program_id(1) * hps
    row = lax.broadcasted_iota(jnp.int32, (SEQ, HEAD_D), 0)
    lane = lax.broadcasted_iota(jnp.int32, (SEQ, HEAD_D), 1)

    ab = ab_ref[...]
    g_all = -jnp.exp(alog_ref[...]) * jax.nn.softplus(ab + dtb_ref[...])
    betaall_s[...] = jax.nn.sigmoid(ab)
    pos = row % CHUNK
    gc = g_all
    step = 1
    while step < CHUNK:
        gc = gc + jnp.where(pos >= step, pltpu.roll(gc, step, axis=0), 0.0)
        step *= 2
    gcall_s[...] = gc

    r_i = lax.broadcasted_iota(jnp.int32, (SUPER, SUPER), 0)
    c_i = lax.broadcasted_iota(jnp.int32, (SUPER, SUPER), 1)
    same = (r_i // CHUNK) == (c_i // CHUNK)
    incl = same & (r_i >= c_i)
    strict = same & (r_i > c_i)
    eye = (r_i == c_i).astype(F32)
    first_half = r_i < CHUNK

    def phase1(it, carry, hd):
        sbs = [it * nb + s for s in range(nb)]
        rows = [pl.ds(pl.multiple_of(sb * SUPER, SUPER), SUPER) for sb in sbs]
        q = [q_s[r, :] for r in rows]
        k = [k_s[r, :] for r in rows]
        cb = [gc_s[r, :] for r in rows]
        beta = [beta_s[r, :] for r in rows]
        kb = [k[s] * beta[s] for s in range(nb)]
        kk = [_bdot_nt(kb[s], k[s]) for s in range(nb)]
        qk = [_bdot_nt(q[s], k[s]) for s in range(nb)]
        decay = [jnp.exp(jnp.where(incl, cb[s] - cb[s].T, -jnp.inf)) for s in range(nb)]
        p = [-jnp.where(strict, kk[s] * decay[s], 0.0) for s in range(nb)]
        t = [eye + p[s] for s in range(nb)]
        for s in range(nb):
            attn_s[hd, rows[s], :] = (qk[s] * decay[s]).astype(BF16)
        n_sq = 1
        while 2 * n_sq < CHUNK:
            p = [_bdot(p[s], p[s]) for s in range(nb)]
            t = [t[s] + _bdot(t[s], p[s]) for s in range(nb)]
            n_sq *= 2
        egc = [jnp.exp(cb[s]) for s in range(nb)]
        rhs = [jnp.concatenate([v_s[rows[s], :] * beta[s], kb[s] * egc[s]], axis=1)
               for s in range(nb)]
        uk = [_bdot(t[s], rhs[s]) for s in range(nb)]
        gl = [jnp.where(first_half,
                        jnp.broadcast_to(cb[s][CHUNK - 1:CHUNK, :], (SUPER, SUPER)),
                        jnp.broadcast_to(cb[s][SUPER - 1:SUPER, :], (SUPER, SUPER)))
              for s in range(nb)]
        kd = [k[s] * jnp.exp(gl[s] - cb[s]) for s in range(nb)]
        rhs2 = [jnp.concatenate([uk[s][:, HEAD_D:], uk[s][:, :HEAD_D]], axis=1)
                for s in range(nb)]
        mb0 = [_bdot_tn(jnp.where(first_half, kd[s], 0.0), rhs2[s]) for s in range(nb)]
        mb1 = [_bdot_tn(jnp.where(first_half, 0.0, kd[s]), rhs2[s]) for s in range(nb)]
        for s in range(nb):
            u_s[hd, rows[s], :] = uk[s][:, :HEAD_D]
            kcum = uk[s][:, HEAD_D:].astype(BF16)
            qd = (q[s] * egc[s]).astype(BF16)
            egl = jnp.exp(gl[s])
            for half, mb in enumerate((mb0[s], mb1[s])):
                c0 = pl.multiple_of((2 * sbs[s] + half) * SUPER, SUPER)
                lo = half * CHUNK
                kq_s[hd, pl.ds(c0, CHUNK), :] = kcum[lo:lo + CHUNK]
                kq_s[hd, pl.ds(c0 + CHUNK, CHUNK), :] = qd[lo:lo + CHUNK]
                m_s[hd, pl.ds(c0, SUPER), :] = mb[:, :HEAD_D].astype(BF16)
                b_s[hd, pl.ds(c0, SUPER), :] = mb[:, HEAD_D:]
                e0 = pl.multiple_of((2 * sbs[s] + half) * SUBLANES, SUBLANES)
                egl_s[hd, pl.ds(e0, SUBLANES), :] = egl[lo:lo + SUBLANES]
        return carry

    for hd in range(hps):
        cols = slice(hd * HEAD_D, (hd + 1) * HEAD_D)
        q_s[...] = _l2norm(_silu(_causal_conv(xq_ref[:, cols], wq_ref[:, cols], DN_CONV))) * (
            HEAD_D ** -0.5)
        k_s[...] = _l2norm(_silu(_causal_conv(xk_ref[:, cols], wk_ref[:, cols], DN_CONV)))
        v_s[...] = _silu(_causal_conv(xv_ref[:, cols], wv_ref[:, cols], DN_CONV))
        gc_s[...] = _pick_lane(gcall_s[...], lane, head0 + hd)
        beta_s[...] = _pick_lane(betaall_s[...], lane, N_HEADS + head0 + hd)
        lax.fori_loop(0, N_SUPER // nb, functools.partial(phase1, hd=hd), 0)

    def phase2(c, states):
        r = pl.ds(pl.multiple_of(c * SUPER, SUPER), SUPER)
        e = pl.ds(pl.multiple_of(c * SUBLANES, SUBLANES), SUBLANES)
        sb16 = [st.astype(BF16) for st in states]
        ms = [jnp.dot(m_s[hd, r, :], sb16[hd], preferred_element_type=F32) for hd in range(hps)]
        new = []
        for hd in range(hps):
            hist_s[hd, r, :] = sb16[hd]
            egl = egl_s[hd, e, :]
            scaled = (states[hd].reshape(HEAD_D // SUBLANES, SUBLANES, HEAD_D) * egl[None]
                      ).reshape(HEAD_D, HEAD_D)
            new.append(scaled + (b_s[hd, r, :] - ms[hd]))
        return tuple(new)

    lax.fori_loop(0, N_CHUNK, phase2,
                  tuple(jnp.zeros((HEAD_D, HEAD_D), F32) for _ in range(hps)))

    ng = ng_ref[...]

    def phase3(it, carry, hd):
        sbs = [it * nb + s for s in range(nb)]
        rows = [pl.ds(pl.multiple_of(sb * SUPER, SUPER), SUPER) for sb in sbs]
        ks = []
        for s in range(nb):
            for half in range(2):
                c = pl.ds(pl.multiple_of((2 * sbs[s] + half) * SUPER, SUPER), SUPER)
                ks.append(jnp.dot(kq_s[hd, c, :], hist_s[hd, c, :], preferred_element_type=F32))
        vnew, o1 = [], []
        for s in range(nb):
            u = u_s[hd, rows[s], :]
            k0, k1 = ks[2 * s], ks[2 * s + 1]
            vnew.append(jnp.concatenate([u[:CHUNK] - k0[:CHUNK], u[CHUNK:] - k1[:CHUNK]], axis=0))
            o1.append(jnp.concatenate([k0[CHUNK:], k1[CHUNK:]], axis=0))
        o = [o1[s] + jnp.dot(attn_s[hd, rows[s], :], vnew[s].astype(BF16),
                             preferred_element_type=F32) for s in range(nb)]
        for s in range(nb):
            z = z_ref[rows[s], hd * HEAD_D:(hd + 1) * HEAD_D]
            on = (o[s] * _rms_scale(o[s])) * ng
            o_ref[rows[s], hd * HEAD_D:(hd + 1) * HEAD_D] = (on * _silu(z)).astype(o_ref.dtype)
        return carry

    for hd in range(hps):
        lax.fori_loop(0, N_SUPER // nb, functools.partial(phase3, hd=hd), 0)


def deltanet(qkv, rest, ab, conv_w_t, a_log, dt_bias, norm_g):
    hps = HEADS_PER_STEP
    wblk = hps * HEAD_D
    seq_blk = lambda off: pl.BlockSpec((SEQ, wblk), lambda b, h: (b, off // wblk + h))
    w_blk = lambda off: pl.BlockSpec((DN_CONV, wblk), lambda b, h: (0, off // wblk + h))
    vec = pl.BlockSpec((1, HEAD_D), lambda b, h: (0, 0))
    pad = lambda x: jnp.pad(x, (0, HEAD_D - N_HEADS)).reshape(1, HEAD_D)
    f32_seq = pltpu.VMEM((SEQ, HEAD_D), F32)
    per_chunk = lambda dt: pltpu.VMEM((hps, N_CHUNK * SUPER, HEAD_D), dt)
    return pl.pallas_call(
        _deltanet_kernel,
        grid=(BATCH, N_HEADS // hps),
        in_specs=[seq_blk(0), seq_blk(D_DN), seq_blk(2 * D_DN),
                  pl.BlockSpec((SEQ, wblk), lambda b, h: (b, OFF_Z // wblk + h)),
                  pl.BlockSpec((SEQ, HEAD_D), lambda b, h: (b, 0)),
                  w_blk(0), w_blk(D_DN), w_blk(2 * D_DN), vec, vec, vec],
        out_specs=pl.BlockSpec((SEQ, wblk), lambda b, h: (b, h)),
        out_shape=jax.ShapeDtypeStruct((M_TOK, D_DN), BF16),
        scratch_shapes=[f32_seq, f32_seq, f32_seq, f32_seq, f32_seq, f32_seq, f32_seq,
                        per_chunk(BF16),
                        pltpu.VMEM((hps, SEQ, HEAD_D), F32),
                        pltpu.VMEM((hps, SEQ, HEAD_D), BF16),
                        per_chunk(BF16),
                        per_chunk(F32),
                        pltpu.VMEM((hps, N_CHUNK * SUBLANES, HEAD_D), F32),
                        per_chunk(BF16)],
        compiler_params=_cparams(2),
        name="deltanet",
    )(qkv, qkv, qkv, rest, ab, conv_w_t, conv_w_t, conv_w_t,
      pad(a_log), pad(dt_bias), norm_g.reshape(1, HEAD_D))


def _ffn_block(h, xn, w_in, w_out, layer, gpost, gpre_next):
    act = ffn_in(xn, w_in, layer)
    f = matmul(act, w_out, layer=layer, tm=1024, tn=256, name="ffn_out")
    return resid_norm(h, f, gpost, gpre_next, 0.5)


def _mixer_block(h, xn, w_in, layer, dn_conv_w, dn_a_log, dn_dt_bias, dn_norm_g, w_dn_out,
                 sc_conv_w, w_sc_out, w_o, gpost, gpre_next):
    qkv = matmul(xn, w_in, layer=layer, tm=2048, tn=512, n=D_QKV, name="in_proj_qkv")
    ab = matmul(xn, w_in, layer=layer, tm=2048, tn=HEAD_D, col0=D_QKV, n=HEAD_D,
                name="in_proj_ab")
    w_rest = w_in[layer, :, D_QKV + 2 * N_HEADS:].astype(BF16)
    rest = matmul(xn, w_rest, tm=2048, tn=512, name="in_proj_rest")
    o_dn = deltanet(qkv, rest, ab, dn_conv_w.T, dn_a_log, dn_dt_bias, dn_norm_g)
    o_sc = short_conv(rest, sc_conv_w.T)
    merged = merge_branches(o_dn, o_sc, w_dn_out, w_sc_out, layer, rest)
    m = matmul(merged, w_o, layer=layer, tm=1024, tn=512, name="out_proj")
    return resid_norm(h, m, gpost, gpre_next, 1.0)


def kernel(x, ffn1_pre_g, ffn1_post_g, w_ffn1_in, w_ffn1_out, mix_pre_g, mix_post_g, w_in,
           dn_conv_w, dn_a_log, dn_dt_bias, dn_norm_g, w_dn_out, sc_conv_w, w_sc_out, w_o,
           ffn2_pre_g, ffn2_post_g, w_ffn2_in, w_ffn2_out):
    h = x.reshape(M_TOK, D_MODEL)
    xn = rmsnorm_cast(h, ffn1_pre_g[0])
    for l in range(DEPTH):
        h, xn = _ffn_block(h, xn, w_ffn1_in, w_ffn1_out, l, ffn1_post_g[l], mix_pre_g[l])
        h, xn = _mixer_block(h, xn, w_in, l, dn_conv_w[l], dn_a_log[l], dn_dt_bias[l],
                             dn_norm_g[l], w_dn_out, sc_conv_w[l], w_sc_out, w_o,
                             mix_post_g[l], ffn2_pre_g[l])
        nxt = ffn1_pre_g[l + 1] if l + 1 < DEPTH else None
        h, xn = _ffn_block(h, xn, w_ffn2_in, w_ffn2_out, l, ffn2_post_g[l], nxt)
    return h.reshape(BATCH, SEQ, D_MODEL)
```

```python
import functools

import jax
import jax.numpy as jnp
from jax import lax
from jax.experimental import pallas as pl
from jax.experimental.pallas import tpu as pltpu

D_MODEL = 4096
BATCH = 4
SEQ = 2048
DEPTH = 2
CHUNK = 64
N_HEADS = 16
HEAD_D = 128
D_DN = N_HEADS * HEAD_D
D_QKV = 3 * D_DN
DN_CONV = 4
D_SC = D_MODEL // 2
SC_CONV = 3
D_FF = 11008
EPS = 1e-6
M_TOK = BATCH * SEQ

OFF_Z = 0
OFF_SCB = OFF_Z + D_DN
OFF_SCC = OFF_SCB + D_SC
OFF_SCX = OFF_SCC + D_SC
OFF_GA = OFF_SCX + D_SC
OFF_GB = OFF_GA + D_MODEL
D_REST = OFF_GB + D_MODEL

SUPER = 2 * CHUNK
N_SUPER = SEQ // SUPER
N_CHUNK = SEQ // CHUNK
SB_PER_ITER = 8
HEADS_PER_STEP = 2
SUBLANES = 8

VMEM_LIMIT = 62 * 1024 * 1024

F32 = jnp.float32
BF16 = jnp.bfloat16


def _cparams(n_axes):
    return pltpu.CompilerParams(
        dimension_semantics=("arbitrary",) * n_axes, vmem_limit_bytes=VMEM_LIMIT)


def _bdot(a, b):
    return jnp.dot(a.astype(BF16), b.astype(BF16), preferred_element_type=F32)


def _bdot_nt(a, b):
    return lax.dot_general(a.astype(BF16), b.astype(BF16), (((1,), (1,)), ((), ())),
                           preferred_element_type=F32)


def _bdot_tn(a, b):
    return lax.dot_general(a.astype(BF16), b.astype(BF16), (((0,), (0,)), ((), ())),
                           preferred_element_type=F32)


def _rms_scale(x):
    return lax.rsqrt(jnp.mean(x * x, axis=-1, keepdims=True) + EPS)


def _rmsnorm_cast_kernel(x_ref, g_ref, o_ref):
    x = x_ref[...]
    o_ref[...] = ((x * _rms_scale(x)) * g_ref[...]).astype(o_ref.dtype)


def rmsnorm_cast(x, g, *, tr=256):
    m, d = x.shape
    return pl.pallas_call(
        _rmsnorm_cast_kernel,
        grid=(m // tr,),
        in_specs=[pl.BlockSpec((tr, d), lambda i: (i, 0)),
                  pl.BlockSpec((1, d), lambda i: (0, 0))],
        out_specs=pl.BlockSpec((tr, d), lambda i: (i, 0)),
        out_shape=jax.ShapeDtypeStruct((m, d), BF16),
        compiler_params=_cparams(1),
        name="rmsnorm_cast",
    )(x, g.reshape(1, d))


def _resid_norm_kernel(h_ref, f_ref, gpost_ref, gpre_ref, hout_ref, xn_ref, *, scale):
    f = f_ref[...]
    h = h_ref[...] + scale * ((f * _rms_scale(f)) * gpost_ref[...])
    hout_ref[...] = h
    xn_ref[...] = ((h * _rms_scale(h)) * gpre_ref[...]).astype(xn_ref.dtype)


def _resid_kernel(h_ref, f_ref, gpost_ref, hout_ref, *, scale):
    f = f_ref[...]
    hout_ref[...] = h_ref[...] + scale * ((f * _rms_scale(f)) * gpost_ref[...])


def resid_norm(h, f, gpost, gpre, scale, *, tr=256):
    m, d = h.shape
    row = pl.BlockSpec((tr, d), lambda i: (i, 0))
    vec = pl.BlockSpec((1, d), lambda i: (0, 0))
    if gpre is None:
        return pl.pallas_call(
            functools.partial(_resid_kernel, scale=scale),
            grid=(m // tr,),
            in_specs=[row, row, vec],
            out_specs=row,
            out_shape=jax.ShapeDtypeStruct((m, d), F32),
            compiler_params=_cparams(1),
            name="resid_final",
        )(h, f, gpost.reshape(1, d)), None
    return pl.pallas_call(
        functools.partial(_resid_norm_kernel, scale=scale),
        grid=(m // tr,),
        in_specs=[row, row, vec, vec],
        out_specs=[row, row],
        out_shape=[jax.ShapeDtypeStruct((m, d), F32), jax.ShapeDtypeStruct((m, d), BF16)],
        compiler_params=_cparams(1),
        name="resid_norm",
    )(h, f, gpost.reshape(1, d), gpre.reshape(1, d))


def _resident(shape, index_map):
    return pl.BlockSpec(shape, index_map, pipeline_mode=pl.Buffered(1))


def _w_spec(w, k, tn, layer, col_of):
    if w.ndim == 2:
        return pl.BlockSpec((k, tn), lambda i, j: (0, col_of(j)))
    return pl.BlockSpec((None, k, tn), lambda i, j: (layer, 0, col_of(j)))


def _dot(a, b):
    return jnp.dot(a, b, preferred_element_type=F32)


def _fused_kernel(*refs, n_act, n_w, n_extra, compute):
    acts = refs[:n_act]
    weights = refs[n_act:n_act + n_w]
    extras = refs[n_act + n_w:n_act + n_w + n_extra]
    o_ref = refs[n_act + n_w + n_extra]
    bufs = refs[n_act + n_w + n_extra + 1:]
    for w_ref, buf in zip(weights, bufs):
        buf[...] = w_ref[...].astype(BF16)
    out = compute([a[...] for a in acts], [b[...] for b in bufs], [e[...] for e in extras])
    o_ref[...] = out.astype(o_ref.dtype)


def _fused_call(compute, acts, weights, col_ofs, extras, extra_col_ofs, *, layer, tm, tn, nj,
                out_cols, out_dtype, name):
    m = acts[0].shape[0]
    k = weights[0].shape[-2]
    in_specs = [_resident((tm, a.shape[1]), lambda i, j: (i, 0)) for a in acts]
    in_specs += [_w_spec(w, k, tn, layer, c) for w, c in zip(weights, col_ofs)]
    in_specs += [pl.BlockSpec((tm, tn), lambda i, j, c=c: (i, c(j))) for c in extra_col_ofs]
    return pl.pallas_call(
        functools.partial(_fused_kernel, n_act=len(acts), n_w=len(weights),
                          n_extra=len(extras), compute=compute),
        grid=(m // tm, nj),
        in_specs=in_specs,
        out_specs=pl.BlockSpec((tm, tn), lambda i, j: (i, j)),
        out_shape=jax.ShapeDtypeStruct((m, out_cols), out_dtype),
        scratch_shapes=[pltpu.VMEM((k, tn), BF16) for _ in weights],
        compiler_params=_cparams(2),
        name=name,
    )(*acts, *weights, *extras)


def _matmul_kernel(a_ref, b_ref, o_ref):
    o_ref[...] = _dot(a_ref[...], b_ref[...]).astype(o_ref.dtype)


def matmul(a, w, *, tm, tn, layer=None, col0=0, n=None, out_dtype=F32, name="matmul"):
    m, k = a.shape
    n = w.shape[-1] - col0 if n is None else n
    j0 = col0 // tn
    if w.dtype != BF16:
        return _fused_call(lambda acts, ws, extras: _dot(acts[0], ws[0]), [a], [w],
                           [lambda j: j0 + j], [], [], layer=layer, tm=tm, tn=tn, nj=n // tn,
                           out_cols=n, out_dtype=out_dtype, name=name)
    return pl.pallas_call(
        _matmul_kernel,
        grid=(m // tm, n // tn),
        in_specs=[_resident((tm, k), lambda i, j: (i, 0)),
                  _w_spec(w, k, tn, layer, lambda j: j0 + j)],
        out_specs=pl.BlockSpec((tm, tn), lambda i, j: (i, j)),
        out_shape=jax.ShapeDtypeStruct((m, n), out_dtype),
        compiler_params=_cparams(2),
        name=name,
    )(a, w)


def _silu(x):
    return x * jax.nn.sigmoid(x)


def _ffn_in_compute(acts, ws, extras):
    return _silu(_dot(acts[0], ws[0])) * _dot(acts[0], ws[1])


def ffn_in(xn, w, layer, *, tm=4096, tn=256):
    nj = D_FF // tn
    return _fused_call(_ffn_in_compute, [xn], [w, w], [lambda j: j, lambda j: nj + j], [], [],
                       layer=layer, tm=tm, tn=tn, nj=nj, out_cols=D_FF, out_dtype=BF16,
                       name="ffn_in")


def _merge_compute(acts, ws, extras):
    return (jax.nn.sigmoid(extras[0]) * _dot(acts[0], ws[0])
            + jax.nn.sigmoid(extras[1]) * _dot(acts[1], ws[1]))


def merge_branches(oa, ob, wa, wb, layer, rest, *, tm=2048, tn=256):
    n = wa.shape[-1]
    ga0, gb0 = OFF_GA // tn, OFF_GB // tn
    return _fused_call(_merge_compute, [oa, ob], [wa, wb], [lambda j: j, lambda j: j],
                       [rest, rest], [lambda j: ga0 + j, lambda j: gb0 + j],
                       layer=layer, tm=tm, tn=tn, nj=n // tn, out_cols=n, out_dtype=BF16,
                       name="merge_branches")


def _realign_kernel(a_ref, b_ref, o_ref, *, shift):
    x = jnp.concatenate([a_ref[...], b_ref[...]], axis=1)
    o_ref[...] = x[:, shift:shift + o_ref.shape[1]].astype(o_ref.dtype)


def realign_cast(w, layer, col0, n, *, tr=512, tc=2048, lane=128):
    k = w.shape[1]
    base, shift = col0 - col0 % lane, col0 % lane
    return pl.pallas_call(
        functools.partial(_realign_kernel, shift=shift),
        grid=(k // tr, n // tc),
        in_specs=[pl.BlockSpec((None, tr, tc), lambda i, j: (layer, i, base // tc + j)),
                  pl.BlockSpec((None, tr, lane),
                               lambda i, j: (layer, i, (base + tc * (j + 1)) // lane))],
        out_specs=pl.BlockSpec((tr, tc), lambda i, j: (i, j)),
        out_shape=jax.ShapeDtypeStruct((k, n), BF16),
        compiler_params=_cparams(2),
        name="realign_cast",
    )(w, w)


def _shift_rows(x, s):
    r = pltpu.roll(x, s, axis=0)
    row = lax.broadcasted_iota(jnp.int32, (SUBLANES, x.shape[1]), 0)
    head = jnp.where(row >= s, r[0:SUBLANES], 0.0)
    return jnp.concatenate([head, r[SUBLANES:]], axis=0)


def _causal_conv(x, w, width):
    acc = _shift_rows(x, width - 1) * w[0:1, :]
    for j in range(1, width):
        s = width - 1 - j
        term = x if s == 0 else _shift_rows(x, s)
        acc = acc + term * w[j:j + 1, :]
    return acc


def _short_conv_kernel(b_ref, c_ref, x_ref, w_ref, o_ref):
    acc = _causal_conv(c_ref[...] * x_ref[...], w_ref[...], SC_CONV)
    o_ref[...] = (b_ref[...] * acc).astype(o_ref.dtype)


def short_conv(rest, w_t, *, tc=256):
    nb = D_SC // tc
    return pl.pallas_call(
        _short_conv_kernel,
        grid=(BATCH, nb),
        in_specs=[pl.BlockSpec((SEQ, tc), lambda b, j: (b, OFF_SCB // tc + j)),
                  pl.BlockSpec((SEQ, tc), lambda b, j: (b, OFF_SCC // tc + j)),
                  pl.BlockSpec((SEQ, tc), lambda b, j: (b, OFF_SCX // tc + j)),
                  pl.BlockSpec((SC_CONV, tc), lambda b, j: (0, j))],
        out_specs=pl.BlockSpec((SEQ, tc), lambda b, j: (b, j)),
        out_shape=jax.ShapeDtypeStruct((M_TOK, D_SC), BF16),
        compiler_params=_cparams(2),
        name="short_conv",
    )(rest, rest, rest, w_t)


def _l2norm(x):
    return x * lax.rsqrt(jnp.sum(x * x, axis=-1, keepdims=True) + EPS)


def _pick_lane(x, lane, idx):
    picked = jnp.sum(jnp.where(lane == idx, x, 0.0), axis=-1, keepdims=True)
    return jnp.broadcast_to(picked, x.shape)


def _deltanet_kernel(xq_ref, xk_ref, xv_ref, z_ref, ab_ref, wq_ref, wk_ref, wv_ref,
                     alog_ref, dtb_ref, ng_ref, o_ref,
                     q_s, k_s, v_s, gcall_s, betaall_s, gc_s, beta_s,
                     kq_s, u_s, attn_s, m_s, b_s, egl_s, hist_s):
    nb = SB_PER_ITER
    hps = HEADS_PER_STEP
    head0 = pl.program_id(1) * hps
    lane = lax.broadcasted_iota(jnp.int32, (SEQ, HEAD_D), 1)

    ab = ab_ref[...]
    g_all = -jnp.exp(alog_ref[...]) * jax.nn.softplus(ab + dtb_ref[...])
    betaall_s[...] = jax.nn.sigmoid(ab)
    r_i = lax.broadcasted_iota(jnp.int32, (SUPER, SUPER), 0)
    c_i = lax.broadcasted_iota(jnp.int32, (SUPER, SUPER), 1)
    same = (r_i // CHUNK) == (c_i // CHUNK)
    incl = same & (r_i >= c_i)
    strict = same & (r_i > c_i)
    eye = (r_i == c_i).astype(F32)
    first_half = r_i < CHUNK

    tri = incl.astype(BF16)
    tiles = [g_all[t * SUPER:(t + 1) * SUPER] for t in range(N_SUPER)]
    hi = [g.astype(BF16) for g in tiles]
    r1 = [g - h.astype(F32) for g, h in zip(tiles, hi)]
    mid = [r.astype(BF16) for r in r1]
    lo = [(r - m.astype(F32)).astype(BF16) for r, m in zip(r1, mid)]
    for t in range(N_SUPER):
        gcall_s[t * SUPER:(t + 1) * SUPER, :] = (
            (_dot(tri, hi[t]) + _dot(tri, mid[t])) + _dot(tri, lo[t]))

    def phase1(it, carry, hd):
        sbs = [it * nb + s for s in range(nb)]
        rows = [pl.ds(pl.multiple_of(sb * SUPER, SUPER), SUPER) for sb in sbs]
        q = [q_s[r, :] for r in rows]
        k = [k_s[r, :] for r in rows]
        cb = [gc_s[r, :] for r in rows]
        beta = [beta_s[r, :] for r in rows]
        kb = [k[s] * beta[s] for s in range(nb)]
        kk = [_bdot_nt(kb[s], k[s]) for s in range(nb)]
        qk = [_bdot_nt(q[s], k[s]) for s in range(nb)]
        decay = [jnp.exp(jnp.where(incl, cb[s] - cb[s].T, -jnp.inf)) for s in range(nb)]
        p = [-jnp.where(strict, kk[s] * decay[s], 0.0) for s in range(nb)]
        t = [eye + p[s] for s in range(nb)]
        for s in range(nb):
            attn_s[hd, rows[s], :] = (qk[s] * decay[s]).astype(BF16)
        n_sq = 1
        while 2 * n_sq < CHUNK:
            p = [_bdot(p[s], p[s]) for s in range(nb)]
            t = [t[s] + _bdot(t[s], p[s]) for s in range(nb)]
            n_sq *= 2
        egc = [jnp.exp(cb[s]) for s in range(nb)]
        rhs = [jnp.concatenate([v_s[rows[s], :] * beta[s], kb[s] * egc[s]], axis=1)
               for s in range(nb)]
        uk = [_bdot(t[s], rhs[s]) for s in range(nb)]
        gl = [jnp.where(first_half,
                        jnp.broadcast_to(cb[s][CHUNK - 1:CHUNK, :], (SUPER, SUPER)),
                        jnp.broadcast_to(cb[s][SUPER - 1:SUPER, :], (SUPER, SUPER)))
              for s in range(nb)]
        kd = [k[s] * jnp.exp(gl[s] - cb[s]) for s in range(nb)]
        rhs2 = [jnp.concatenate([uk[s][:, HEAD_D:], uk[s][:, :HEAD_D]], axis=1)
                for s in range(nb)]
        mb0 = [_bdot_tn(jnp.where(first_half, kd[s], 0.0), rhs2[s]) for s in range(nb)]
        mb1 = [_bdot_tn(jnp.where(first_half, 0.0, kd[s]), rhs2[s]) for s in range(nb)]
        for s in range(nb):
            u_s[hd, rows[s], :] = uk[s][:, :HEAD_D]
            kcum = uk[s][:, HEAD_D:].astype(BF16)
            qd = (q[s] * egc[s]).astype(BF16)
            egl = jnp.exp(gl[s])
            for half, mb in enumerate((mb0[s], mb1[s])):
                c0 = pl.multiple_of((2 * sbs[s] + half) * SUPER, SUPER)
                lo = half * CHUNK
                kq_s[hd, pl.ds(c0, CHUNK), :] = kcum[lo:lo + CHUNK]
                kq_s[hd, pl.ds(c0 + CHUNK, CHUNK), :] = qd[lo:lo + CHUNK]
                m_s[hd, pl.ds(c0, SUPER), :] = mb[:, :HEAD_D].astype(BF16)
                b_s[hd, pl.ds(c0, SUPER), :] = mb[:, HEAD_D:]
                e0 = pl.multiple_of((2 * sbs[s] + half) * SUBLANES, SUBLANES)
                egl_s[hd, pl.ds(e0, SUBLANES), :] = egl[lo:lo + SUBLANES]
        return carry

    for hd in range(hps):
        cols = slice(hd * HEAD_D, (hd + 1) * HEAD_D)
        q_s[...] = _l2norm(_silu(_causal_conv(xq_ref[:, cols], wq_ref[:, cols], DN_CONV))) * (
            HEAD_D ** -0.5)
        k_s[...] = _l2norm(_silu(_causal_conv(xk_ref[:, cols], wk_ref[:, cols], DN_CONV)))
        v_s[...] = _silu(_causal_conv(xv_ref[:, cols], wv_ref[:, cols], DN_CONV))
        gc_s[...] = _pick_lane(gcall_s[...], lane, head0 + hd)
        beta_s[...] = _pick_lane(betaall_s[...], lane, N_HEADS + head0 + hd)
        lax.fori_loop(0, N_SUPER // nb, functools.partial(phase1, hd=hd), 0)

    def phase2(c, states):
        r = pl.ds(pl.multiple_of(c * SUPER, SUPER), SUPER)
        e = pl.ds(pl.multiple_of(c * SUBLANES, SUBLANES), SUBLANES)
        sb16 = [st.astype(BF16) for st in states]
        ms = [jnp.dot(m_s[hd, r, :], sb16[hd], preferred_element_type=F32) for hd in range(hps)]
        new = []
        for hd in range(hps):
            hist_s[hd, r, :] = sb16[hd]
            egl = egl_s[hd, e, :]
            scaled = (states[hd].reshape(HEAD_D // SUBLANES, SUBLANES, HEAD_D) * egl[None]
                      ).reshape(HEAD_D, HEAD_D)
            new.append(scaled + (b_s[hd, r, :] - ms[hd]))
        return tuple(new)

    lax.fori_loop(0, N_CHUNK, phase2,
                  tuple(jnp.zeros((HEAD_D, HEAD_D), F32) for _ in range(hps)))

    ng = ng_ref[...]

    def phase3(it, carry, hd):
        sbs = [it * nb + s for s in range(nb)]
        rows = [pl.ds(pl.multiple_of(sb * SUPER, SUPER), SUPER) for sb in sbs]
        ks = []
        for s in range(nb):
            for half in range(2):
                c = pl.ds(pl.multiple_of((2 * sbs[s] + half) * SUPER, SUPER), SUPER)
                ks.append(jnp.dot(kq_s[hd, c, :], hist_s[hd, c, :], preferred_element_type=F32))
        vnew, o1 = [], []
        for s in range(nb):
            u = u_s[hd, rows[s], :]
            k0, k1 = ks[2 * s], ks[2 * s + 1]
            vnew.append(jnp.concatenate([u[:CHUNK] - k0[:CHUNK], u[CHUNK:] - k1[:CHUNK]], axis=0))
            o1.append(jnp.concatenate([k0[CHUNK:], k1[CHUNK:]], axis=0))
        o = [o1[s] + jnp.dot(attn_s[hd, rows[s], :], vnew[s].astype(BF16),
                             preferred_element_type=F32) for s in range(nb)]
        for s in range(nb):
            z = z_ref[rows[s], hd * HEAD_D:(hd + 1) * HEAD_D]
            on = (o[s] * _rms_scale(o[s])) * ng
            o_ref[rows[s], hd * HEAD_D:(hd + 1) * HEAD_D] = (on * _silu(z)).astype(o_ref.dtype)
        return carry

    for hd in range(hps):
        lax.fori_loop(0, N_SUPER // nb, functools.partial(phase3, hd=hd), 0)


def deltanet(qkv, rest, ab, conv_w_t, a_log, dt_bias, norm_g):
    hps = HEADS_PER_STEP
    wblk = hps * HEAD_D
    seq_blk = lambda off: pl.BlockSpec((SEQ, wblk), lambda b, h: (b, off // wblk + h))
    w_blk = lambda off: pl.BlockSpec((DN_CONV, wblk), lambda b, h: (0, off // wblk + h))
    vec = pl.BlockSpec((1, HEAD_D), lambda b, h: (0, 0))
    pad = lambda x: jnp.pad(x, (0, HEAD_D - N_HEADS)).reshape(1, HEAD_D)
    f32_seq = pltpu.VMEM((SEQ, HEAD_D), F32)
    per_chunk = lambda dt: pltpu.VMEM((hps, N_CHUNK * SUPER, HEAD_D), dt)
    return pl.pallas_call(
        _deltanet_kernel,
        grid=(BATCH, N_HEADS // hps),
        in_specs=[seq_blk(0), seq_blk(D_DN), seq_blk(2 * D_DN),
                  pl.BlockSpec((SEQ, wblk), lambda b, h: (b, OFF_Z // wblk + h)),
                  pl.BlockSpec((SEQ, HEAD_D), lambda b, h: (b, 0)),
                  w_blk(0), w_blk(D_DN), w_blk(2 * D_DN), vec, vec, vec],
        out_specs=pl.BlockSpec((SEQ, wblk), lambda b, h: (b, h)),
        out_shape=jax.ShapeDtypeStruct((M_TOK, D_DN), BF16),
        scratch_shapes=[f32_seq, f32_seq, f32_seq, f32_seq, f32_seq, f32_seq, f32_seq,
                        per_chunk(BF16),
                        pltpu.VMEM((hps, SEQ, HEAD_D), F32),
                        pltpu.VMEM((hps, SEQ, HEAD_D), BF16),
                        per_chunk(BF16),
                        per_chunk(F32),
                        pltpu.VMEM((hps, N_CHUNK * SUBLANES, HEAD_D), F32),
                        per_chunk(BF16)],
        compiler_params=_cparams(2),
        name="deltanet",
    )(qkv, qkv, qkv, rest, ab, conv_w_t, conv_w_t, conv_w_t,
      pad(a_log), pad(dt_bias), norm_g.reshape(1, HEAD_D))


def _ffn_block(h, xn, w_in, w_out, layer, gpost, gpre_next):
    act = ffn_in(xn, w_in, layer)
    f = matmul(act, w_out, layer=layer, tm=1024, tn=256, name="ffn_out")
    return resid_norm(h, f, gpost, gpre_next, 0.5)


def _mixer_block(h, xn, w_in, layer, dn_conv_w, dn_a_log, dn_dt_bias, dn_norm_g, w_dn_out,
                 sc_conv_w, w_sc_out, w_o, gpost, gpre_next):
    qkv = matmul(xn, w_in, layer=layer, tm=2048, tn=512, n=D_QKV, name="in_proj_qkv")
    ab = matmul(xn, w_in, layer=layer, tm=2048, tn=HEAD_D, col0=D_QKV, n=HEAD_D,
                name="in_proj_ab")
    w_rest = realign_cast(w_in, layer, D_QKV + 2 * N_HEADS, D_REST)
    rest = matmul(xn, w_rest, tm=2048, tn=512, name="in_proj_rest")
    o_dn = deltanet(qkv, rest, ab, dn_conv_w.T, dn_a_log, dn_dt_bias, dn_norm_g)
    o_sc = short_conv(rest, sc_conv_w.T)
    merged = merge_branches(o_dn, o_sc, w_dn_out, w_sc_out, layer, rest)
    m = matmul(merged, w_o, layer=layer, tm=2048, tn=512, name="out_proj")
    return resid_norm(h, m, gpost, gpre_next, 1.0)


def kernel(x, ffn1_pre_g, ffn1_post_g, w_ffn1_in, w_ffn1_out, mix_pre_g, mix_post_g, w_in,
           dn_conv_w, dn_a_log, dn_dt_bias, dn_norm_g, w_dn_out, sc_conv_w, w_sc_out, w_o,
           ffn2_pre_g, ffn2_post_g, w_ffn2_in, w_ffn2_out):
    h = x.reshape(M_TOK, D_MODEL)
    xn = rmsnorm_cast(h, ffn1_pre_g[0])
    for l in range(DEPTH):
        h, xn = _ffn_block(h, xn, w_ffn1_in, w_ffn1_out, l, ffn1_post_g[l], mix_pre_g[l])
        h, xn = _mixer_block(h, xn, w_in, l, dn_conv_w[l], dn_a_log[l], dn_dt_bias[l],
                             dn_norm_g[l], w_dn_out, sc_conv_w[l], w_sc_out, w_o,
                             mix_post_g[l], ffn2_pre_g[l])
        nxt = ffn1_pre_g[l + 1] if l + 1 < DEPTH else None
        h, xn = _ffn_block(h, xn, w_ffn2_in, w_ffn2_out, l, ffn2_post_g[l], nxt)
    return h.reshape(BATCH, SEQ, D_MODEL)
```

```python
import functools

import jax
import jax.numpy as jnp
from jax import lax
from jax.experimental import pallas as pl
from jax.experimental.pallas import tpu as pltpu

D_MODEL = 4096
BATCH = 4
SEQ = 2048
DEPTH = 2
CHUNK = 64
N_HEADS = 16
HEAD_D = 128
D_DN = N_HEADS * HEAD_D
D_QKV = 3 * D_DN
DN_CONV = 4
D_SC = D_MODEL // 2
SC_CONV = 3
D_FF = 11008
EPS = 1e-6
M_TOK = BATCH * SEQ

OFF_Z = 0
OFF_SCB = OFF_Z + D_DN
OFF_SCC = OFF_SCB + D_SC
OFF_SCX = OFF_SCC + D_SC
OFF_GA = OFF_SCX + D_SC
OFF_GB = OFF_GA + D_MODEL
D_REST = OFF_GB + D_MODEL

SUPER = 2 * CHUNK
N_SUPER = SEQ // SUPER
N_CHUNK = SEQ // CHUNK
SB_PER_ITER = 8
HEADS_PER_STEP = 2
SUBLANES = 8

VMEM_LIMIT = 62 * 1024 * 1024

F32 = jnp.float32
BF16 = jnp.bfloat16


def _cparams(n_axes):
    return pltpu.CompilerParams(
        dimension_semantics=("arbitrary",) * n_axes, vmem_limit_bytes=VMEM_LIMIT)


def _bdot(a, b):
    return jnp.dot(a.astype(BF16), b.astype(BF16), preferred_element_type=F32)


def _bdot_nt(a, b):
    return lax.dot_general(a.astype(BF16), b.astype(BF16), (((1,), (1,)), ((), ())),
                           preferred_element_type=F32)


def _bdot_tn(a, b):
    return lax.dot_general(a.astype(BF16), b.astype(BF16), (((0,), (0,)), ((), ())),
                           preferred_element_type=F32)


def _rms_scale(x):
    return lax.rsqrt(jnp.mean(x * x, axis=-1, keepdims=True) + EPS)


def _rmsnorm_cast_kernel(x_ref, g_ref, o_ref):
    x = x_ref[...]
    o_ref[...] = ((x * _rms_scale(x)) * g_ref[...]).astype(o_ref.dtype)


def rmsnorm_cast(x, g, *, tr=256):
    m, d = x.shape
    return pl.pallas_call(
        _rmsnorm_cast_kernel,
        grid=(m // tr,),
        in_specs=[pl.BlockSpec((tr, d), lambda i: (i, 0)),
                  pl.BlockSpec((1, d), lambda i: (0, 0))],
        out_specs=pl.BlockSpec((tr, d), lambda i: (i, 0)),
        out_shape=jax.ShapeDtypeStruct((m, d), BF16),
        compiler_params=_cparams(1),
        name="rmsnorm_cast",
    )(x, g.reshape(1, d))


def _resid_norm_kernel(h_ref, f_ref, gpost_ref, gpre_ref, hout_ref, xn_ref, *, scale):
    f = f_ref[...]
    h = h_ref[...] + scale * ((f * _rms_scale(f)) * gpost_ref[...])
    hout_ref[...] = h
    xn_ref[...] = ((h * _rms_scale(h)) * gpre_ref[...]).astype(xn_ref.dtype)


def _resid_kernel(h_ref, f_ref, gpost_ref, hout_ref, *, scale):
    f = f_ref[...]
    hout_ref[...] = h_ref[...] + scale * ((f * _rms_scale(f)) * gpost_ref[...])


def resid_norm(h, f, gpost, gpre, scale, *, tr=256):
    m, d = h.shape
    row = pl.BlockSpec((tr, d), lambda i: (i, 0))
    vec = pl.BlockSpec((1, d), lambda i: (0, 0))
    if gpre is None:
        return pl.pallas_call(
            functools.partial(_resid_kernel, scale=scale),
            grid=(m // tr,),
            in_specs=[row, row, vec],
            out_specs=row,
            out_shape=jax.ShapeDtypeStruct((m, d), F32),
            compiler_params=_cparams(1),
            name="resid_final",
        )(h, f, gpost.reshape(1, d)), None
    return pl.pallas_call(
        functools.partial(_resid_norm_kernel, scale=scale),
        grid=(m // tr,),
        in_specs=[row, row, vec, vec],
        out_specs=[row, row],
        out_shape=[jax.ShapeDtypeStruct((m, d), F32), jax.ShapeDtypeStruct((m, d), BF16)],
        compiler_params=_cparams(1),
        name="resid_norm",
    )(h, f, gpost.reshape(1, d), gpre.reshape(1, d))


def _resident(shape, index_map):
    return pl.BlockSpec(shape, index_map, pipeline_mode=pl.Buffered(1))


def _w_spec(w, k, tn, layer, col_of):
    if w.ndim == 2:
        return pl.BlockSpec((k, tn), lambda i, j: (0, col_of(j)))
    return pl.BlockSpec((None, k, tn), lambda i, j: (layer, 0, col_of(j)))


def _dot(a, b):
    return jnp.dot(a, b, preferred_element_type=F32)


def _fused_kernel(*refs, n_act, n_w, n_extra, compute):
    acts = refs[:n_act]
    weights = refs[n_act:n_act + n_w]
    extras = refs[n_act + n_w:n_act + n_w + n_extra]
    o_ref = refs[n_act + n_w + n_extra]
    bufs = refs[n_act + n_w + n_extra + 1:]
    for w_ref, buf in zip(weights, bufs):
        buf[...] = w_ref[...].astype(BF16)
    out = compute([a[...] for a in acts], [b[...] for b in bufs], [e[...] for e in extras])
    o_ref[...] = out.astype(o_ref.dtype)


def _fused_call(compute, acts, weights, col_ofs, extras, extra_col_ofs, *, layer, tm, tn, nj,
                out_cols, out_dtype, name):
    m = acts[0].shape[0]
    k = weights[0].shape[-2]
    in_specs = [_resident((tm, a.shape[1]), lambda i, j: (i, 0)) for a in acts]
    in_specs += [_w_spec(w, k, tn, layer, c) for w, c in zip(weights, col_ofs)]
    in_specs += [pl.BlockSpec((tm, tn), lambda i, j, c=c: (i, c(j))) for c in extra_col_ofs]
    return pl.pallas_call(
        functools.partial(_fused_kernel, n_act=len(acts), n_w=len(weights),
                          n_extra=len(extras), compute=compute),
        grid=(m // tm, nj),
        in_specs=in_specs,
        out_specs=pl.BlockSpec((tm, tn), lambda i, j: (i, j)),
        out_shape=jax.ShapeDtypeStruct((m, out_cols), out_dtype),
        scratch_shapes=[pltpu.VMEM((k, tn), BF16) for _ in weights],
        compiler_params=_cparams(2),
        name=name,
    )(*acts, *weights, *extras)


def _matmul_kernel(a_ref, b_ref, o_ref):
    o_ref[...] = _dot(a_ref[...], b_ref[...]).astype(o_ref.dtype)


def matmul(a, w, *, tm, tn, layer=None, col0=0, n=None, out_dtype=F32, name="matmul"):
    m, k = a.shape
    n = w.shape[-1] - col0 if n is None else n
    j0 = col0 // tn
    if w.dtype != BF16:
        return _fused_call(lambda acts, ws, extras: _dot(acts[0], ws[0]), [a], [w],
                           [lambda j: j0 + j], [], [], layer=layer, tm=tm, tn=tn, nj=n // tn,
                           out_cols=n, out_dtype=out_dtype, name=name)
    return pl.pallas_call(
        _matmul_kernel,
        grid=(m // tm, n // tn),
        in_specs=[_resident((tm, k), lambda i, j: (i, 0)),
                  _w_spec(w, k, tn, layer, lambda j: j0 + j)],
        out_specs=pl.BlockSpec((tm, tn), lambda i, j: (i, j)),
        out_shape=jax.ShapeDtypeStruct((m, n), out_dtype),
        compiler_params=_cparams(2),
        name=name,
    )(a, w)


def _sigmoid(x):
    return 1.0 / (1.0 + jnp.exp(-x))


def _silu(x):
    return x * _sigmoid(x)


def _ffn_in_compute(acts, ws, extras):
    return _silu(_dot(acts[0], ws[0])) * _dot(acts[0], ws[1])


def ffn_in(xn, w, layer, *, tm=4096, tn=256):
    nj = D_FF // tn
    return _fused_call(_ffn_in_compute, [xn], [w, w], [lambda j: j, lambda j: nj + j], [], [],
                       layer=layer, tm=tm, tn=tn, nj=nj, out_cols=D_FF, out_dtype=BF16,
                       name="ffn_in")


def _merge_compute(acts, ws, extras):
    return (_sigmoid(extras[0]) * _dot(acts[0], ws[0])
            + _sigmoid(extras[1]) * _dot(acts[1], ws[1]))


def merge_branches(oa, ob, wa, wb, layer, rest, *, tm=2048, tn=256):
    n = wa.shape[-1]
    ga0, gb0 = OFF_GA // tn, OFF_GB // tn
    return _fused_call(_merge_compute, [oa, ob], [wa, wb], [lambda j: j, lambda j: j],
                       [rest, rest], [lambda j: ga0 + j, lambda j: gb0 + j],
                       layer=layer, tm=tm, tn=tn, nj=n // tn, out_cols=n, out_dtype=BF16,
                       name="merge_branches")


def _matmul_nt_kernel(a_ref, wt_ref, o_ref, wb_ref):
    wb_ref[...] = wt_ref[0].astype(BF16)
    o_ref[...] = lax.dot_general(a_ref[...], wb_ref[...], (((1,), (1,)), ((), ())),
                                 preferred_element_type=F32).astype(o_ref.dtype)


def matmul_nt(a, wt, *, layer, row0, n, tm, tn, name):
    m, k = a.shape
    assert row0 % SUBLANES == 0
    return pl.pallas_call(
        _matmul_nt_kernel,
        grid=(m // tm, n // tn),
        in_specs=[_resident((tm, k), lambda i, j: (i, 0)),
                  pl.BlockSpec((pl.Element(1), pl.Element(tn), pl.Element(k)),
                               lambda i, j: (layer, pl.multiple_of(row0 + j * tn, SUBLANES), 0))],
        out_specs=pl.BlockSpec((tm, tn), lambda i, j: (i, j)),
        out_shape=jax.ShapeDtypeStruct((m, n), F32),
        scratch_shapes=[pltpu.VMEM((tn, k), BF16)],
        compiler_params=_cparams(2),
        name=name,
    )(a, wt)


def _shift_rows(x, s):
    r = pltpu.roll(x, s, axis=0)
    row = lax.broadcasted_iota(jnp.int32, (SUBLANES, x.shape[1]), 0)
    head = jnp.where(row >= s, r[0:SUBLANES], 0.0)
    return jnp.concatenate([head, r[SUBLANES:]], axis=0)


def _causal_conv(x, w, width):
    acc = _shift_rows(x, width - 1) * w[0:1, :]
    for j in range(1, width):
        s = width - 1 - j
        term = x if s == 0 else _shift_rows(x, s)
        acc = acc + term * w[j:j + 1, :]
    return acc


def _shifted_load(x_ref, cols, s):
    n = x_ref.shape[0]
    row = lax.broadcasted_iota(jnp.int32, (SUBLANES, cols.stop - cols.start), 0)
    head = jnp.where(row >= s, pltpu.roll(x_ref[0:SUBLANES, cols], s, axis=0), 0.0)
    return jnp.concatenate([head, x_ref[SUBLANES - s:n - s, cols]], axis=0)


def _causal_conv_ref(x_ref, cols, w, width):
    acc = _shifted_load(x_ref, cols, width - 1) * w[0:1, :]
    for j in range(1, width):
        s = width - 1 - j
        term = x_ref[:, cols] if s == 0 else _shifted_load(x_ref, cols, s)
        acc = acc + term * w[j:j + 1, :]
    return acc


def _short_conv_kernel(b_ref, c_ref, x_ref, w_ref, o_ref):
    acc = _causal_conv(c_ref[...] * x_ref[...], w_ref[...], SC_CONV)
    o_ref[...] = (b_ref[...] * acc).astype(o_ref.dtype)


def short_conv(rest, w_t, *, tc=256):
    nb = D_SC // tc
    return pl.pallas_call(
        _short_conv_kernel,
        grid=(BATCH, nb),
        in_specs=[pl.BlockSpec((SEQ, tc), lambda b, j: (b, OFF_SCB // tc + j)),
                  pl.BlockSpec((SEQ, tc), lambda b, j: (b, OFF_SCC // tc + j)),
                  pl.BlockSpec((SEQ, tc), lambda b, j: (b, OFF_SCX // tc + j)),
                  pl.BlockSpec((SC_CONV, tc), lambda b, j: (0, j))],
        out_specs=pl.BlockSpec((SEQ, tc), lambda b, j: (b, j)),
        out_shape=jax.ShapeDtypeStruct((M_TOK, D_SC), BF16),
        compiler_params=_cparams(2),
        name="short_conv",
    )(rest, rest, rest, w_t)


def _l2norm(x):
    return x * lax.rsqrt(jnp.sum(x * x, axis=-1, keepdims=True) + EPS)


def _pick_lane(x, lane, idx):
    picked = jnp.sum(jnp.where(lane == idx, x, 0.0), axis=-1, keepdims=True)
    return jnp.broadcast_to(picked, x.shape)


def _deltanet_kernel(xq_ref, xk_ref, xv_ref, z_ref, ab_ref, wq_ref, wk_ref, wv_ref,
                     alog_ref, dtb_ref, ng_ref, o_ref,
                     q_s, k_s, v_s, gcall_s, betaall_s, gc_s, beta_s,
                     kq_s, u_s, attn_s, m_s, b_s, egl_s, hist_s):
    nb = SB_PER_ITER
    hps = HEADS_PER_STEP
    head0 = pl.program_id(1) * hps
    lane = lax.broadcasted_iota(jnp.int32, (SEQ, HEAD_D), 1)

    ab = ab_ref[...]
    g_all = -jnp.exp(alog_ref[...]) * jax.nn.softplus(ab + dtb_ref[...])
    betaall_s[...] = _sigmoid(ab)
    r_i = lax.broadcasted_iota(jnp.int32, (SUPER, SUPER), 0)
    c_i = lax.broadcasted_iota(jnp.int32, (SUPER, SUPER), 1)
    same = (r_i // CHUNK) == (c_i // CHUNK)
    incl = same & (r_i >= c_i)
    strict = same & (r_i > c_i)
    eye = (r_i == c_i).astype(F32)
    first_half = r_i < CHUNK

    tri = incl.astype(BF16)
    tiles = [g_all[t * SUPER:(t + 1) * SUPER] for t in range(N_SUPER)]
    hi = [g.astype(BF16) for g in tiles]
    r1 = [g - h.astype(F32) for g, h in zip(tiles, hi)]
    mid = [r.astype(BF16) for r in r1]
    lo = [(r - m.astype(F32)).astype(BF16) for r, m in zip(r1, mid)]
    for t in range(N_SUPER):
        gcall_s[t * SUPER:(t + 1) * SUPER, :] = (
            (_dot(tri, hi[t]) + _dot(tri, mid[t])) + _dot(tri, lo[t]))

    def phase1(it, carry, hd):
        sbs = [it * nb + s for s in range(nb)]
        rows = [pl.ds(pl.multiple_of(sb * SUPER, SUPER), SUPER) for sb in sbs]
        q = [q_s[r, :] for r in rows]
        k = [k_s[r, :] for r in rows]
        cb = [gc_s[r, :] for r in rows]
        beta = [beta_s[r, :] for r in rows]
        kb = [k[s] * beta[s] for s in range(nb)]
        kk = [_bdot_nt(kb[s], k[s]) for s in range(nb)]
        qk = [_bdot_nt(q[s], k[s]) for s in range(nb)]
        decay = [jnp.exp(jnp.where(incl, cb[s] - cb[s].T, -jnp.inf)) for s in range(nb)]
        p = [-jnp.where(strict, kk[s] * decay[s], 0.0) for s in range(nb)]
        t = [eye + p[s] for s in range(nb)]
        for s in range(nb):
            attn_s[hd, rows[s], :] = (qk[s] * decay[s]).astype(BF16)
        n_sq = 1
        while 2 * n_sq < CHUNK:
            p = [_bdot(p[s], p[s]) for s in range(nb)]
            t = [t[s] + _bdot(t[s], p[s]) for s in range(nb)]
            n_sq *= 2
        egc = [jnp.exp(cb[s]) for s in range(nb)]
        rhs = [jnp.concatenate([v_s[rows[s], :] * beta[s], kb[s] * egc[s]], axis=1)
               for s in range(nb)]
        uk = [_bdot(t[s], rhs[s]) for s in range(nb)]
        gl = [jnp.where(first_half,
                        jnp.broadcast_to(cb[s][CHUNK - 1:CHUNK, :], (SUPER, SUPER)),
                        jnp.broadcast_to(cb[s][SUPER - 1:SUPER, :], (SUPER, SUPER)))
              for s in range(nb)]
        kd = [k[s] * jnp.exp(gl[s] - cb[s]) for s in range(nb)]
        rhs2 = [jnp.concatenate([uk[s][:, HEAD_D:], uk[s][:, :HEAD_D]], axis=1)
                for s in range(nb)]
        mb0 = [_bdot_tn(jnp.where(first_half, kd[s], 0.0), rhs2[s]) for s in range(nb)]
        mb1 = [_bdot_tn(jnp.where(first_half, 0.0, kd[s]), rhs2[s]) for s in range(nb)]
        for s in range(nb):
            u_s[hd, rows[s], :] = uk[s][:, :HEAD_D]
            kcum = uk[s][:, HEAD_D:].astype(BF16)
            qd = (q[s] * egc[s]).astype(BF16)
            egl = jnp.exp(gl[s])
            for half, mb in enumerate((mb0[s], mb1[s])):
                c0 = pl.multiple_of((2 * sbs[s] + half) * SUPER, SUPER)
                lo = half * CHUNK
                kq_s[hd, pl.ds(c0, CHUNK), :] = kcum[lo:lo + CHUNK]
                kq_s[hd, pl.ds(c0 + CHUNK, CHUNK), :] = qd[lo:lo + CHUNK]
                m_s[hd, pl.ds(c0, SUPER), :] = mb[:, :HEAD_D].astype(BF16)
                b_s[hd, pl.ds(c0, SUPER), :] = mb[:, HEAD_D:]
                e0 = pl.multiple_of((2 * sbs[s] + half) * SUBLANES, SUBLANES)
                egl_s[hd, pl.ds(e0, SUBLANES), :] = egl[lo:lo + SUBLANES]
        return carry

    for hd in range(hps):
        cols = slice(hd * HEAD_D, (hd + 1) * HEAD_D)
        q_s[...] = _l2norm(_silu(_causal_conv_ref(xq_ref, cols, wq_ref[:, cols], DN_CONV))) * (
            HEAD_D ** -0.5)
        k_s[...] = _l2norm(_silu(_causal_conv_ref(xk_ref, cols, wk_ref[:, cols], DN_CONV)))
        v_s[...] = _silu(_causal_conv_ref(xv_ref, cols, wv_ref[:, cols], DN_CONV))
        gc_s[...] = _pick_lane(gcall_s[...], lane, head0 + hd)
        beta_s[...] = _pick_lane(betaall_s[...], lane, N_HEADS + head0 + hd)
        lax.fori_loop(0, N_SUPER // nb, functools.partial(phase1, hd=hd), 0)

    def phase2(c, states):
        r = pl.ds(pl.multiple_of(c * SUPER, SUPER), SUPER)
        e = pl.ds(pl.multiple_of(c * SUBLANES, SUBLANES), SUBLANES)
        sb16 = [st.astype(BF16) for st in states]
        ms = [jnp.dot(m_s[hd, r, :], sb16[hd], preferred_element_type=F32) for hd in range(hps)]
        new = []
        for hd in range(hps):
            hist_s[hd, r, :] = sb16[hd]
            egl = egl_s[hd, e, :]
            scaled = (states[hd].reshape(HEAD_D // SUBLANES, SUBLANES, HEAD_D) * egl[None]
                      ).reshape(HEAD_D, HEAD_D)
            new.append(scaled + (b_s[hd, r, :] - ms[hd]))
        return tuple(new)

    lax.fori_loop(0, N_CHUNK, phase2,
                  tuple(jnp.zeros((HEAD_D, HEAD_D), F32) for _ in range(hps)))

    ng = ng_ref[...]

    def phase3(it, carry, hd):
        sbs = [it * nb + s for s in range(nb)]
        rows = [pl.ds(pl.multiple_of(sb * SUPER, SUPER), SUPER) for sb in sbs]
        ks = []
        for s in range(nb):
            for half in range(2):
                c = pl.ds(pl.multiple_of((2 * sbs[s] + half) * SUPER, SUPER), SUPER)
                ks.append(jnp.dot(kq_s[hd, c, :], hist_s[hd, c, :], preferred_element_type=F32))
        vnew, o1 = [], []
        for s in range(nb):
            u = u_s[hd, rows[s], :]
            k0, k1 = ks[2 * s], ks[2 * s + 1]
            vnew.append(jnp.concatenate([u[:CHUNK] - k0[:CHUNK], u[CHUNK:] - k1[:CHUNK]], axis=0))
            o1.append(jnp.concatenate([k0[CHUNK:], k1[CHUNK:]], axis=0))
        o = [o1[s] + jnp.dot(attn_s[hd, rows[s], :], vnew[s].astype(BF16),
                             preferred_element_type=F32) for s in range(nb)]
        for s in range(nb):
            z = z_ref[rows[s], hd * HEAD_D:(hd + 1) * HEAD_D]
            on = (o[s] * _rms_scale(o[s])) * ng
            o_ref[rows[s], hd * HEAD_D:(hd + 1) * HEAD_D] = (on * _silu(z)).astype(o_ref.dtype)
        return carry

    for hd in range(hps):
        lax.fori_loop(0, N_SUPER // nb, functools.partial(phase3, hd=hd), 0)


def deltanet(qkv, rest, ab, conv_w_t, a_log, dt_bias, norm_g):
    hps = HEADS_PER_STEP
    wblk = hps * HEAD_D
    seq_blk = lambda off: pl.BlockSpec((SEQ, wblk), lambda b, h: (b, off // wblk + h))
    w_blk = lambda off: pl.BlockSpec((DN_CONV, wblk), lambda b, h: (0, off // wblk + h))
    vec = pl.BlockSpec((1, HEAD_D), lambda b, h: (0, 0))
    pad = lambda x: jnp.pad(x, (0, HEAD_D - N_HEADS)).reshape(1, HEAD_D)
    f32_seq = pltpu.VMEM((SEQ, HEAD_D), F32)
    per_chunk = lambda dt: pltpu.VMEM((hps, N_CHUNK * SUPER, HEAD_D), dt)
    return pl.pallas_call(
        _deltanet_kernel,
        grid=(BATCH, N_HEADS // hps),
        in_specs=[seq_blk(0), seq_blk(D_DN), seq_blk(2 * D_DN),
                  pl.BlockSpec((SEQ, wblk), lambda b, h: (b, OFF_Z // wblk + h)),
                  pl.BlockSpec((SEQ, HEAD_D), lambda b, h: (b, 0)),
                  w_blk(0), w_blk(D_DN), w_blk(2 * D_DN), vec, vec, vec],
        out_specs=pl.BlockSpec((SEQ, wblk), lambda b, h: (b, h)),
        out_shape=jax.ShapeDtypeStruct((M_TOK, D_DN), BF16),
        scratch_shapes=[f32_seq, f32_seq, f32_seq, f32_seq, f32_seq, f32_seq, f32_seq,
                        per_chunk(BF16),
                        pltpu.VMEM((hps, SEQ, HEAD_D), F32),
                        pltpu.VMEM((hps, SEQ, HEAD_D), BF16),
                        per_chunk(BF16),
                        per_chunk(F32),
                        pltpu.VMEM((hps, N_CHUNK * SUBLANES, HEAD_D), F32),
                        per_chunk(BF16)],
        compiler_params=_cparams(2),
        name="deltanet",
    )(qkv, qkv, qkv, rest, ab, conv_w_t, conv_w_t, conv_w_t,
      pad(a_log), pad(dt_bias), norm_g.reshape(1, HEAD_D))


def _ffn_block(h, xn, w_in, w_out, layer, gpost, gpre_next):
    act = ffn_in(xn, w_in, layer)
    f = matmul(act, w_out, layer=layer, tm=1024, tn=256, name="ffn_out")
    return resid_norm(h, f, gpost, gpre_next, 0.5)


def _mixer_block(h, xn, w_in, layer, dn_conv_w, dn_a_log, dn_dt_bias, dn_norm_g, w_dn_out,
                 sc_conv_w, w_sc_out, w_o, gpost, gpre_next):
    wt = jnp.swapaxes(w_in, 1, 2)
    proj = functools.partial(matmul_nt, xn, wt, layer=layer, tm=2048)
    qkv = proj(row0=0, n=D_QKV, tn=512, name="in_proj_qkv")
    ab = proj(row0=D_QKV, n=HEAD_D, tn=HEAD_D, name="in_proj_ab")
    rest = proj(row0=D_QKV + 2 * N_HEADS, n=D_REST, tn=512, name="in_proj_rest")
    o_dn = deltanet(qkv, rest, ab, dn_conv_w.T, dn_a_log, dn_dt_bias, dn_norm_g)
    o_sc = short_conv(rest, sc_conv_w.T)
    merged = merge_branches(o_dn, o_sc, w_dn_out, w_sc_out, layer, rest)
    m = matmul(merged, w_o, layer=layer, tm=2048, tn=512, name="out_proj")
    return resid_norm(h, m, gpost, gpre_next, 1.0)


def kernel(x, ffn1_pre_g, ffn1_post_g, w_ffn1_in, w_ffn1_out, mix_pre_g, mix_post_g, w_in,
           dn_conv_w, dn_a_log, dn_dt_bias, dn_norm_g, w_dn_out, sc_conv_w, w_sc_out, w_o,
           ffn2_pre_g, ffn2_post_g, w_ffn2_in, w_ffn2_out):
    h = x.reshape(M_TOK, D_MODEL)
    xn = rmsnorm_cast(h, ffn1_pre_g[0])
    for l in range(DEPTH):
        h, xn = _ffn_block(h, xn, w_ffn1_in, w_ffn1_out, l, ffn1_post_g[l], mix_pre_g[l])
        h, xn = _mixer_block(h, xn, w_in, l, dn_conv_w[l], dn_a_log[l], dn_dt_bias[l],
                             dn_norm_g[l], w_dn_out, sc_conv_w[l], w_sc_out, w_o,
                             mix_post_g[l], ffn2_pre_g[l])
        nxt = ffn1_pre_g[l + 1] if l + 1 < DEPTH else None
        h, xn = _ffn_block(h, xn, w_ffn2_in, w_ffn2_out, l, ffn2_post_g[l], nxt)
    return h.reshape(BATCH, SEQ, D_MODEL)
```

```python
import functools

import jax
import jax.numpy as jnp
from jax import lax
from jax.experimental import pallas as pl
from jax.experimental.pallas import tpu as pltpu

D_MODEL = 4096
BATCH = 4
SEQ = 2048
DEPTH = 2
CHUNK = 64
N_HEADS = 16
HEAD_D = 128
D_DN = N_HEADS * HEAD_D
D_QKV = 3 * D_DN
DN_CONV = 4
D_SC = D_MODEL // 2
SC_CONV = 3
D_FF = 11008
EPS = 1e-6
M_TOK = BATCH * SEQ

OFF_Z = 0
OFF_SCB = OFF_Z + D_DN
OFF_SCC = OFF_SCB + D_SC
OFF_SCX = OFF_SCC + D_SC
OFF_GA = OFF_SCX + D_SC
OFF_GB = OFF_GA + D_MODEL
D_REST = OFF_GB + D_MODEL

SUPER = 2 * CHUNK
N_SUPER = SEQ // SUPER
N_CHUNK = SEQ // CHUNK
SB_PER_ITER = 16
HEADS_PER_STEP = 2
SUBLANES = 8

VMEM_LIMIT = 62 * 1024 * 1024

F32 = jnp.float32
BF16 = jnp.bfloat16


def _cparams(n_axes):
    return pltpu.CompilerParams(
        dimension_semantics=("arbitrary",) * n_axes, vmem_limit_bytes=VMEM_LIMIT)


def _bdot(a, b):
    return jnp.dot(a.astype(BF16), b.astype(BF16), preferred_element_type=F32)


def _bdot_nt(a, b):
    return lax.dot_general(a.astype(BF16), b.astype(BF16), (((1,), (1,)), ((), ())),
                           preferred_element_type=F32)


def _bdot_tn(a, b):
    return lax.dot_general(a.astype(BF16), b.astype(BF16), (((0,), (0,)), ((), ())),
                           preferred_element_type=F32)


def _rms_scale(x):
    return lax.rsqrt(jnp.mean(x * x, axis=-1, keepdims=True) + EPS)


def _rmsnorm_cast_kernel(x_ref, g_ref, o_ref):
    x = x_ref[...]
    o_ref[...] = ((x * _rms_scale(x)) * g_ref[...]).astype(o_ref.dtype)


def rmsnorm_cast(x, g, *, tr=256):
    m, d = x.shape
    return pl.pallas_call(
        _rmsnorm_cast_kernel,
        grid=(m // tr,),
        in_specs=[pl.BlockSpec((tr, d), lambda i: (i, 0)),
                  pl.BlockSpec((1, d), lambda i: (0, 0))],
        out_specs=pl.BlockSpec((tr, d), lambda i: (i, 0)),
        out_shape=jax.ShapeDtypeStruct((m, d), BF16),
        compiler_params=_cparams(1),
        name="rmsnorm_cast",
    )(x, g.reshape(1, d))


def _resid_norm_kernel(h_ref, f_ref, gpost_ref, gpre_ref, hout_ref, xn_ref, *, scale):
    f = f_ref[...]
    h = h_ref[...] + scale * ((f * _rms_scale(f)) * gpost_ref[...])
    hout_ref[...] = h
    xn_ref[...] = ((h * _rms_scale(h)) * gpre_ref[...]).astype(xn_ref.dtype)


def _resid_kernel(h_ref, f_ref, gpost_ref, hout_ref, *, scale):
    f = f_ref[...]
    hout_ref[...] = h_ref[...] + scale * ((f * _rms_scale(f)) * gpost_ref[...])


def resid_norm(h, f, gpost, gpre, scale, *, tr=256):
    m, d = h.shape
    row = pl.BlockSpec((tr, d), lambda i: (i, 0))
    vec = pl.BlockSpec((1, d), lambda i: (0, 0))
    if gpre is None:
        return pl.pallas_call(
            functools.partial(_resid_kernel, scale=scale),
            grid=(m // tr,),
            in_specs=[row, row, vec],
            out_specs=row,
            out_shape=jax.ShapeDtypeStruct((m, d), F32),
            compiler_params=_cparams(1),
            name="resid_final",
        )(h, f, gpost.reshape(1, d)), None
    return pl.pallas_call(
        functools.partial(_resid_norm_kernel, scale=scale),
        grid=(m // tr,),
        in_specs=[row, row, vec, vec],
        out_specs=[row, row],
        out_shape=[jax.ShapeDtypeStruct((m, d), F32), jax.ShapeDtypeStruct((m, d), BF16)],
        compiler_params=_cparams(1),
        name="resid_norm",
    )(h, f, gpost.reshape(1, d), gpre.reshape(1, d))


def _resident(shape, index_map):
    return pl.BlockSpec(shape, index_map, pipeline_mode=pl.Buffered(1))


def _w_spec(w, k, tn, layer, col_of):
    if w.ndim == 2:
        return pl.BlockSpec((k, tn), lambda i, j: (0, col_of(j)))
    return pl.BlockSpec((None, k, tn), lambda i, j: (layer, 0, col_of(j)))


def _dot(a, b):
    return jnp.dot(a, b, preferred_element_type=F32)


def _fused_kernel(*refs, n_act, n_w, n_extra, compute):
    acts = refs[:n_act]
    weights = refs[n_act:n_act + n_w]
    extras = refs[n_act + n_w:n_act + n_w + n_extra]
    o_ref = refs[n_act + n_w + n_extra]
    bufs = refs[n_act + n_w + n_extra + 1:]
    for w_ref, buf in zip(weights, bufs):
        buf[...] = w_ref[...].astype(BF16)
    out = compute([a[...] for a in acts], [b[...] for b in bufs], [e[...] for e in extras])
    o_ref[...] = out.astype(o_ref.dtype)


def _fused_call(compute, acts, weights, col_ofs, extras, extra_col_ofs, *, layer, tm, tn, nj,
                out_cols, out_dtype, name):
    m = acts[0].shape[0]
    k = weights[0].shape[-2]
    in_specs = [_resident((tm, a.shape[1]), lambda i, j: (i, 0)) for a in acts]
    in_specs += [_w_spec(w, k, tn, layer, c) for w, c in zip(weights, col_ofs)]
    in_specs += [pl.BlockSpec((tm, tn), lambda i, j, c=c: (i, c(j))) for c in extra_col_ofs]
    return pl.pallas_call(
        functools.partial(_fused_kernel, n_act=len(acts), n_w=len(weights),
                          n_extra=len(extras), compute=compute),
        grid=(m // tm, nj),
        in_specs=in_specs,
        out_specs=pl.BlockSpec((tm, tn), lambda i, j: (i, j)),
        out_shape=jax.ShapeDtypeStruct((m, out_cols), out_dtype),
        scratch_shapes=[pltpu.VMEM((k, tn), BF16) for _ in weights],
        compiler_params=_cparams(2),
        name=name,
    )(*acts, *weights, *extras)


def _matmul_kernel(a_ref, b_ref, o_ref):
    o_ref[...] = _dot(a_ref[...], b_ref[...]).astype(o_ref.dtype)


def matmul(a, w, *, tm, tn, layer=None, col0=0, n=None, out_dtype=F32, name="matmul"):
    m, k = a.shape
    n = w.shape[-1] - col0 if n is None else n
    j0 = col0 // tn
    if w.dtype != BF16:
        return _fused_call(lambda acts, ws, extras: _dot(acts[0], ws[0]), [a], [w],
                           [lambda j: j0 + j], [], [], layer=layer, tm=tm, tn=tn, nj=n // tn,
                           out_cols=n, out_dtype=out_dtype, name=name)
    return pl.pallas_call(
        _matmul_kernel,
        grid=(m // tm, n // tn),
        in_specs=[_resident((tm, k), lambda i, j: (i, 0)),
                  _w_spec(w, k, tn, layer, lambda j: j0 + j)],
        out_specs=pl.BlockSpec((tm, tn), lambda i, j: (i, j)),
        out_shape=jax.ShapeDtypeStruct((m, n), out_dtype),
        compiler_params=_cparams(2),
        name=name,
    )(a, w)


def _sigmoid(x):
    return 1.0 / (1.0 + jnp.exp(-x))


def _silu(x):
    return x * _sigmoid(x)


def _ffn_in_compute(acts, ws, extras):
    return _silu(_dot(acts[0], ws[0])) * _dot(acts[0], ws[1])


def ffn_in(xn, w, layer, *, tm=4096, tn=256):
    nj = D_FF // tn
    return _fused_call(_ffn_in_compute, [xn], [w, w], [lambda j: j, lambda j: nj + j], [], [],
                       layer=layer, tm=tm, tn=tn, nj=nj, out_cols=D_FF, out_dtype=BF16,
                       name="ffn_in")


def _merge_compute(acts, ws, extras):
    return (_sigmoid(extras[0]) * _dot(acts[0], ws[0])
            + _sigmoid(extras[1]) * _dot(acts[1], ws[1]))


def merge_branches(oa, ob, wa, wb, layer, rest, *, tm=2048, tn=256):
    n = wa.shape[-1]
    ga0, gb0 = OFF_GA // tn, OFF_GB // tn
    return _fused_call(_merge_compute, [oa, ob], [wa, wb], [lambda j: j, lambda j: j],
                       [rest, rest], [lambda j: ga0 + j, lambda j: gb0 + j],
                       layer=layer, tm=tm, tn=tn, nj=n // tn, out_cols=n, out_dtype=BF16,
                       name="merge_branches")


def _dot_nt(a, b):
    return lax.dot_general(a, b, (((1,), (1,)), ((), ())), preferred_element_type=F32)


def _matmul_nt_kernel(a_ref, wt_ref, *rest):
    o_ref, wb_ref = rest[-2:]
    wb_ref[...] = wt_ref[0].astype(BF16)
    o_ref[...] = _dot_nt(a_ref[...], wb_ref[...])
    if len(rest) == 4:
        side_w_ref, side_o_ref = rest[:2]

        @pl.when(pl.program_id(1) == 0)
        def _():
            side_o_ref[...] = _dot_nt(a_ref[...], side_w_ref[0].astype(BF16))


def matmul_nt(a, wt, *, layer, row0, n, tm, tn, name, side=None):
    m, k = a.shape
    assert row0 % SUBLANES == 0
    w_rows = lambda rows, start: pl.BlockSpec(
        (pl.Element(1), pl.Element(rows), pl.Element(k)),
        lambda i, j: (layer, pl.multiple_of(start(j), SUBLANES), 0))
    in_specs = [_resident((tm, k), lambda i, j: (i, 0)), w_rows(tn, lambda j: row0 + j * tn)]
    out_specs = [pl.BlockSpec((tm, tn), lambda i, j: (i, j))]
    out_shape = [jax.ShapeDtypeStruct((m, n), F32)]
    operands = [a, wt]
    if side is not None:
        side_row0, side_n = side
        assert side_row0 % SUBLANES == 0
        in_specs.insert(2, w_rows(side_n, lambda j: side_row0 + 0 * j))
        out_specs.insert(0, pl.BlockSpec((tm, side_n), lambda i, j: (i, 0)))
        out_shape.insert(0, jax.ShapeDtypeStruct((m, side_n), F32))
        operands.append(wt)
    outs = pl.pallas_call(
        _matmul_nt_kernel,
        grid=(m // tm, n // tn),
        in_specs=in_specs,
        out_specs=out_specs,
        out_shape=out_shape,
        scratch_shapes=[pltpu.VMEM((tn, k), BF16)],
        compiler_params=_cparams(2),
        name=name,
    )(*operands)
    return outs[0] if side is None else (outs[1], outs[0])


def _shift_rows(x, s):
    r = pltpu.roll(x, s, axis=0)
    row = lax.broadcasted_iota(jnp.int32, (SUBLANES, x.shape[1]), 0)
    head = jnp.where(row >= s, r[0:SUBLANES], 0.0)
    return jnp.concatenate([head, r[SUBLANES:]], axis=0)


def _causal_conv(x, w, width):
    acc = _shift_rows(x, width - 1) * w[0:1, :]
    for j in range(1, width):
        s = width - 1 - j
        term = x if s == 0 else _shift_rows(x, s)
        acc = acc + term * w[j:j + 1, :]
    return acc


def _causal_conv_tile(x_ref, cols, w, r0, n_rows, width):
    halo0 = pl.multiple_of(jnp.maximum(r0 - SUBLANES, 0), SUBLANES)
    halo = jnp.where(r0 > 0, x_ref[pl.ds(halo0, SUBLANES), cols], 0.0)
    ext = jnp.concatenate([halo, x_ref[pl.ds(r0, n_rows), cols]], axis=0)
    acc = None
    for j in range(width):
        s = width - 1 - j
        term = ext[SUBLANES - s:SUBLANES - s + n_rows] * w[j:j + 1, :]
        acc = term if acc is None else acc + term
    return acc


def _short_conv_kernel(b_ref, c_ref, x_ref, w_ref, o_ref):
    acc = _causal_conv(c_ref[...] * x_ref[...], w_ref[...], SC_CONV)
    o_ref[...] = (b_ref[...] * acc).astype(o_ref.dtype)


def short_conv(rest, w_t, *, tc=256):
    nb = D_SC // tc
    return pl.pallas_call(
        _short_conv_kernel,
        grid=(BATCH, nb),
        in_specs=[pl.BlockSpec((SEQ, tc), lambda b, j: (b, OFF_SCB // tc + j)),
                  pl.BlockSpec((SEQ, tc), lambda b, j: (b, OFF_SCC // tc + j)),
                  pl.BlockSpec((SEQ, tc), lambda b, j: (b, OFF_SCX // tc + j)),
                  pl.BlockSpec((SC_CONV, tc), lambda b, j: (0, j))],
        out_specs=pl.BlockSpec((SEQ, tc), lambda b, j: (b, j)),
        out_shape=jax.ShapeDtypeStruct((M_TOK, D_SC), BF16),
        compiler_params=_cparams(2),
        name="short_conv",
    )(rest, rest, rest, w_t)


def _l2norm(x):
    return x * lax.rsqrt(jnp.sum(x * x, axis=-1, keepdims=True) + EPS)


def _pick_lane(x, lane, idx):
    picked = jnp.sum(jnp.where(lane == idx, x, 0.0), axis=-1, keepdims=True)
    return jnp.broadcast_to(picked, x.shape)


def _deltanet_kernel(xq_ref, xk_ref, xv_ref, z_ref, ab_ref, wq_ref, wk_ref, wv_ref,
                     alog_ref, dtb_ref, ng_ref, o_ref,
                     gcall_s, betaall_s, kq_s, u_s, attn_s, m_s, b_s, egl_s, hist_s):
    nb = SB_PER_ITER
    hps = HEADS_PER_STEP
    head0 = pl.program_id(1) * hps

    ab = ab_ref[...]
    g_all = -jnp.exp(alog_ref[...]) * jax.nn.softplus(ab + dtb_ref[...])
    betaall_s[...] = _sigmoid(ab)
    r_i = lax.broadcasted_iota(jnp.int32, (SUPER, SUPER), 0)
    c_i = lax.broadcasted_iota(jnp.int32, (SUPER, SUPER), 1)
    same = (r_i // CHUNK) == (c_i // CHUNK)
    incl = same & (r_i >= c_i)
    strict = same & (r_i > c_i)
    eye = (r_i == c_i).astype(F32)
    first_half = r_i < CHUNK

    tri = incl.astype(BF16)
    tiles = [g_all[t * SUPER:(t + 1) * SUPER] for t in range(N_SUPER)]
    hi = [g.astype(BF16) for g in tiles]
    r1 = [g - h.astype(F32) for g, h in zip(tiles, hi)]
    mid = [r.astype(BF16) for r in r1]
    lo = [(r - m.astype(F32)).astype(BF16) for r, m in zip(r1, mid)]
    for t in range(N_SUPER):
        gcall_s[t * SUPER:(t + 1) * SUPER, :] = (
            (_dot(tri, hi[t]) + _dot(tri, mid[t])) + _dot(tri, lo[t]))

    def phase1(it, carry, hd):
        cols = slice(hd * HEAD_D, (hd + 1) * HEAD_D)
        sbs = [it * nb + s for s in range(nb)]
        r0s = [pl.multiple_of(sb * SUPER, SUPER) for sb in sbs]
        rows = [pl.ds(r0, SUPER) for r0 in r0s]

        def conv_silu(x_ref, w_ref, r0):
            return _silu(_causal_conv_tile(x_ref, cols, w_ref[:, cols], r0, SUPER, DN_CONV))

        q = [_l2norm(conv_silu(xq_ref, wq_ref, r0)) * (HEAD_D ** -0.5) for r0 in r0s]
        k = [_l2norm(conv_silu(xk_ref, wk_ref, r0)) for r0 in r0s]
        v = [conv_silu(xv_ref, wv_ref, r0) for r0 in r0s]
        cb = [_pick_lane(gcall_s[r, :], c_i, head0 + hd) for r in rows]
        beta = [_pick_lane(betaall_s[r, :], c_i, N_HEADS + head0 + hd) for r in rows]
        kb = [k[s] * beta[s] for s in range(nb)]
        kk = [_bdot_nt(kb[s], k[s]) for s in range(nb)]
        qk = [_bdot_nt(q[s], k[s]) for s in range(nb)]
        decay = [jnp.exp(jnp.where(incl, cb[s] - cb[s].T, -jnp.inf)) for s in range(nb)]
        p = [-jnp.where(strict, kk[s] * decay[s], 0.0) for s in range(nb)]
        t = [eye + p[s] for s in range(nb)]
        for s in range(nb):
            attn_s[hd, rows[s], :] = (qk[s] * decay[s]).astype(BF16)
        n_sq = 1
        while 2 * n_sq < CHUNK:
            p = [_bdot(p[s], p[s]) for s in range(nb)]
            t = [t[s] + _bdot(t[s], p[s]) for s in range(nb)]
            n_sq *= 2
        egc = [jnp.exp(cb[s]) for s in range(nb)]
        rhs = [jnp.concatenate([v[s] * beta[s], kb[s] * egc[s]], axis=1) for s in range(nb)]
        uk = [_bdot(t[s], rhs[s]) for s in range(nb)]
        gl = [jnp.where(first_half,
                        jnp.broadcast_to(cb[s][CHUNK - 1:CHUNK, :], (SUPER, SUPER)),
                        jnp.broadcast_to(cb[s][SUPER - 1:SUPER, :], (SUPER, SUPER)))
              for s in range(nb)]
        kd = [k[s] * jnp.exp(gl[s] - cb[s]) for s in range(nb)]
        rhs2 = [jnp.concatenate([uk[s][:, HEAD_D:], uk[s][:, :HEAD_D]], axis=1)
                for s in range(nb)]
        mb0 = [_bdot_tn(jnp.where(first_half, kd[s], 0.0), rhs2[s]) for s in range(nb)]
        mb1 = [_bdot_tn(jnp.where(first_half, 0.0, kd[s]), rhs2[s]) for s in range(nb)]
        for s in range(nb):
            u_s[hd, rows[s], :] = uk[s][:, :HEAD_D]
            kcum = uk[s][:, HEAD_D:].astype(BF16)
            qd = (q[s] * egc[s]).astype(BF16)
            egl = jnp.exp(gl[s])
            for half, mb in enumerate((mb0[s], mb1[s])):
                c0 = pl.multiple_of((2 * sbs[s] + half) * SUPER, SUPER)
                lo = half * CHUNK
                kq_s[hd, pl.ds(c0, CHUNK), :] = kcum[lo:lo + CHUNK]
                kq_s[hd, pl.ds(c0 + CHUNK, CHUNK), :] = qd[lo:lo + CHUNK]
                m_s[hd, pl.ds(c0, SUPER), :] = mb[:, :HEAD_D].astype(BF16)
                b_s[hd, pl.ds(c0, SUPER), :] = mb[:, HEAD_D:]
                e0 = pl.multiple_of((2 * sbs[s] + half) * SUBLANES, SUBLANES)
                egl_s[hd, pl.ds(e0, SUBLANES), :] = egl[lo:lo + SUBLANES]
        return carry

    for hd in range(hps):
        lax.fori_loop(0, N_SUPER // nb, functools.partial(phase1, hd=hd), 0)

    def phase2(c, states):
        r = pl.ds(pl.multiple_of(c * SUPER, SUPER), SUPER)
        e = pl.ds(pl.multiple_of(c * SUBLANES, SUBLANES), SUBLANES)
        sb16 = [st.astype(BF16) for st in states]
        ms = [jnp.dot(m_s[hd, r, :], sb16[hd], preferred_element_type=F32) for hd in range(hps)]
        new = []
        for hd in range(hps):
            hist_s[hd, r, :] = sb16[hd]
            egl = egl_s[hd, e, :]
            scaled = (states[hd].reshape(HEAD_D // SUBLANES, SUBLANES, HEAD_D) * egl[None]
                      ).reshape(HEAD_D, HEAD_D)
            new.append(scaled + (b_s[hd, r, :] - ms[hd]))
        return tuple(new)

    lax.fori_loop(0, N_CHUNK, phase2,
                  tuple(jnp.zeros((HEAD_D, HEAD_D), F32) for _ in range(hps)))

    ng = ng_ref[...]

    def phase3(it, carry, hd):
        sbs = [it * nb + s for s in range(nb)]
        rows = [pl.ds(pl.multiple_of(sb * SUPER, SUPER), SUPER) for sb in sbs]
        ks = []
        for s in range(nb):
            for half in range(2):
                c = pl.ds(pl.multiple_of((2 * sbs[s] + half) * SUPER, SUPER), SUPER)
                ks.append(jnp.dot(kq_s[hd, c, :], hist_s[hd, c, :], preferred_element_type=F32))
        vnew, o1 = [], []
        for s in range(nb):
            u = u_s[hd, rows[s], :]
            k0, k1 = ks[2 * s], ks[2 * s + 1]
            vnew.append(jnp.concatenate([u[:CHUNK] - k0[:CHUNK], u[CHUNK:] - k1[:CHUNK]], axis=0))
            o1.append(jnp.concatenate([k0[CHUNK:], k1[CHUNK:]], axis=0))
        o = [o1[s] + jnp.dot(attn_s[hd, rows[s], :], vnew[s].astype(BF16),
                             preferred_element_type=F32) for s in range(nb)]
        for s in range(nb):
            z = z_ref[rows[s], hd * HEAD_D:(hd + 1) * HEAD_D]
            on = (o[s] * _rms_scale(o[s])) * ng
            o_ref[rows[s], hd * HEAD_D:(hd + 1) * HEAD_D] = (on * _silu(z)).astype(o_ref.dtype)
        return carry

    for hd in range(hps):
        lax.fori_loop(0, N_SUPER // nb, functools.partial(phase3, hd=hd), 0)


def deltanet(qkv, rest, ab, conv_w_t, a_log, dt_bias, norm_g):
    hps = HEADS_PER_STEP
    wblk = hps * HEAD_D
    seq_blk = lambda off: pl.BlockSpec((SEQ, wblk), lambda b, h: (b, off // wblk + h))
    w_blk = lambda off: pl.BlockSpec((DN_CONV, wblk), lambda b, h: (0, off // wblk + h))
    vec = pl.BlockSpec((1, HEAD_D), lambda b, h: (0, 0))
    pad = lambda x: jnp.pad(x, (0, HEAD_D - N_HEADS)).reshape(1, HEAD_D)
    f32_seq = pltpu.VMEM((SEQ, HEAD_D), F32)
    per_chunk = lambda dt: pltpu.VMEM((hps, N_CHUNK * SUPER, HEAD_D), dt)
    return pl.pallas_call(
        _deltanet_kernel,
        grid=(BATCH, N_HEADS // hps),
        in_specs=[seq_blk(0), seq_blk(D_DN), seq_blk(2 * D_DN),
                  pl.BlockSpec((SEQ, wblk), lambda b, h: (b, OFF_Z // wblk + h)),
                  pl.BlockSpec((SEQ, HEAD_D), lambda b, h: (b, 0)),
                  w_blk(0), w_blk(D_DN), w_blk(2 * D_DN), vec, vec, vec],
        out_specs=pl.BlockSpec((SEQ, wblk), lambda b, h: (b, h)),
        out_shape=jax.ShapeDtypeStruct((M_TOK, D_DN), BF16),
        scratch_shapes=[f32_seq, f32_seq,
                        per_chunk(BF16),
                        pltpu.VMEM((hps, SEQ, HEAD_D), F32),
                        pltpu.VMEM((hps, SEQ, HEAD_D), BF16),
                        per_chunk(BF16),
                        per_chunk(F32),
                        pltpu.VMEM((hps, N_CHUNK * SUBLANES, HEAD_D), F32),
                        per_chunk(BF16)],
        compiler_params=_cparams(2),
        name="deltanet",
    )(qkv, qkv, qkv, rest, ab, conv_w_t, conv_w_t, conv_w_t,
      pad(a_log), pad(dt_bias), norm_g.reshape(1, HEAD_D))


def _ffn_block(h, xn, w_in, w_out, layer, gpost, gpre_next):
    act = ffn_in(xn, w_in, layer)
    f = matmul(act, w_out, layer=layer, tm=1024, tn=256, name="ffn_out")
    return resid_norm(h, f, gpost, gpre_next, 0.5)


def _mixer_block(h, xn, w_in, layer, dn_conv_w, dn_a_log, dn_dt_bias, dn_norm_g, w_dn_out,
                 sc_conv_w, w_sc_out, w_o, gpost, gpre_next):
    wt = jnp.swapaxes(w_in, 1, 2)
    proj = functools.partial(matmul_nt, xn, wt, layer=layer, tm=2048)
    qkv, ab = proj(row0=0, n=D_QKV, tn=512, name="in_proj_qkv", side=(D_QKV, HEAD_D))
    rest = proj(row0=D_QKV + 2 * N_HEADS, n=D_REST, tn=512, name="in_proj_rest")
    o_dn = deltanet(qkv, rest, ab, dn_conv_w.T, dn_a_log, dn_dt_bias, dn_norm_g)
    o_sc = short_conv(rest, sc_conv_w.T)
    merged = merge_branches(o_dn, o_sc, w_dn_out, w_sc_out, layer, rest)
    m = matmul(merged, w_o, layer=layer, tm=2048, tn=512, name="out_proj")
    return resid_norm(h, m, gpost, gpre_next, 1.0)


def kernel(x, ffn1_pre_g, ffn1_post_g, w_ffn1_in, w_ffn1_out, mix_pre_g, mix_post_g, w_in,
           dn_conv_w, dn_a_log, dn_dt_bias, dn_norm_g, w_dn_out, sc_conv_w, w_sc_out, w_o,
           ffn2_pre_g, ffn2_post_g, w_ffn2_in, w_ffn2_out):
    h = x.reshape(M_TOK, D_MODEL)
    xn = rmsnorm_cast(h, ffn1_pre_g[0])
    for l in range(DEPTH):
        h, xn = _ffn_block(h, xn, w_ffn1_in, w_ffn1_out, l, ffn1_post_g[l], mix_pre_g[l])
        h, xn = _mixer_block(h, xn, w_in, l, dn_conv_w[l], dn_a_log[l], dn_dt_bias[l],
                             dn_norm_g[l], w_dn_out, sc_conv_w[l], w_sc_out, w_o,
                             mix_post_g[l], ffn2_pre_g[l])
        nxt = ffn1_pre_g[l + 1] if l + 1 < DEPTH else None
        h, xn = _ffn_block(h, xn, w_ffn2_in, w_ffn2_out, l, ffn2_post_g[l], nxt)
    return h.reshape(BATCH, SEQ, D_MODEL)
```

```python
import functools

import jax
import jax.numpy as jnp
from jax import lax
from jax.experimental import pallas as pl
from jax.experimental.pallas import tpu as pltpu

D_MODEL = 4096
BATCH = 4
SEQ = 2048
DEPTH = 2
CHUNK = 64
N_HEADS = 16
HEAD_D = 128
D_DN = N_HEADS * HEAD_D
D_QKV = 3 * D_DN
DN_CONV = 4
D_SC = D_MODEL // 2
SC_CONV = 3
D_FF = 11008
EPS = 1e-6
M_TOK = BATCH * SEQ

OFF_Z = 0
OFF_SCB = OFF_Z + D_DN
OFF_SCC = OFF_SCB + D_SC
OFF_SCX = OFF_SCC + D_SC
OFF_GA = OFF_SCX + D_SC
OFF_GB = OFF_GA + D_MODEL
D_REST = OFF_GB + D_MODEL

SUPER = 2 * CHUNK
N_SUPER = SEQ // SUPER
N_CHUNK = SEQ // CHUNK
SB_PER_ITER = 16
HEADS_PER_STEP = 2
SUBLANES = 8

VMEM_LIMIT = 64 * 1024 * 1024

F32 = jnp.float32
BF16 = jnp.bfloat16


def _cparams(n_axes):
    return pltpu.CompilerParams(
        dimension_semantics=("arbitrary",) * n_axes, vmem_limit_bytes=VMEM_LIMIT)


def _bdot(a, b):
    return jnp.dot(a.astype(BF16), b.astype(BF16), preferred_element_type=F32)


def _bdot_nt(a, b):
    return lax.dot_general(a.astype(BF16), b.astype(BF16), (((1,), (1,)), ((), ())),
                           preferred_element_type=F32)


def _bdot_tn(a, b):
    return lax.dot_general(a.astype(BF16), b.astype(BF16), (((0,), (0,)), ((), ())),
                           preferred_element_type=F32)


def _rms_scale(x):
    return lax.rsqrt(jnp.mean(x * x, axis=-1, keepdims=True) + EPS)


def _rmsnorm_cast_kernel(x_ref, g_ref, o_ref):
    x = x_ref[...]
    o_ref[...] = ((x * _rms_scale(x)) * g_ref[...]).astype(o_ref.dtype)


def rmsnorm_cast(x, g, *, tr=256):
    m, d = x.shape
    return pl.pallas_call(
        _rmsnorm_cast_kernel,
        grid=(m // tr,),
        in_specs=[pl.BlockSpec((tr, d), lambda i: (i, 0)),
                  pl.BlockSpec((1, d), lambda i: (0, 0))],
        out_specs=pl.BlockSpec((tr, d), lambda i: (i, 0)),
        out_shape=jax.ShapeDtypeStruct((m, d), BF16),
        compiler_params=_cparams(1),
        name="rmsnorm_cast",
    )(x, g.reshape(1, d))


def _resid_norm_kernel(h_ref, f_ref, gpost_ref, gpre_ref, hout_ref, xn_ref, *, scale):
    f = f_ref[...]
    h = h_ref[...] + scale * ((f * _rms_scale(f)) * gpost_ref[...])
    hout_ref[...] = h
    xn_ref[...] = ((h * _rms_scale(h)) * gpre_ref[...]).astype(xn_ref.dtype)


def _resid_kernel(h_ref, f_ref, gpost_ref, hout_ref, *, scale):
    f = f_ref[...]
    hout_ref[...] = h_ref[...] + scale * ((f * _rms_scale(f)) * gpost_ref[...])


def resid_norm(h, f, gpost, gpre, scale, *, tr=256):
    m, d = h.shape
    row = pl.BlockSpec((tr, d), lambda i: (i, 0))
    vec = pl.BlockSpec((1, d), lambda i: (0, 0))
    if gpre is None:
        return pl.pallas_call(
            functools.partial(_resid_kernel, scale=scale),
            grid=(m // tr,),
            in_specs=[row, row, vec],
            out_specs=row,
            out_shape=jax.ShapeDtypeStruct((m, d), F32),
            compiler_params=_cparams(1),
            name="resid_final",
        )(h, f, gpost.reshape(1, d)), None
    return pl.pallas_call(
        functools.partial(_resid_norm_kernel, scale=scale),
        grid=(m // tr,),
        in_specs=[row, row, vec, vec],
        out_specs=[row, row],
        out_shape=[jax.ShapeDtypeStruct((m, d), F32), jax.ShapeDtypeStruct((m, d), BF16)],
        compiler_params=_cparams(1),
        name="resid_norm",
    )(h, f, gpost.reshape(1, d), gpre.reshape(1, d))


def _resident(shape, index_map):
    return pl.BlockSpec(shape, index_map, pipeline_mode=pl.Buffered(1))


def _w_spec(w, k, tn, layer, col_of):
    if w.ndim == 2:
        return pl.BlockSpec((k, tn), lambda i, j: (0, col_of(j)))
    return pl.BlockSpec((None, k, tn), lambda i, j: (layer, 0, col_of(j)))


def _dot(a, b):
    return jnp.dot(a, b, preferred_element_type=F32)


ACT_CHUNKS = 2


def _fused_kernel(*refs, n_act, n_w, n_extra, tm, compute, chunk_dim):
    weights = refs[n_act:n_act + n_w]
    extras = refs[n_act + n_w:n_act + n_w + n_extra]
    o_ref = refs[n_act + n_w + n_extra]
    scratch = refs[n_act + n_w + n_extra + 1:]
    bufs = scratch[:n_w]
    acts = refs[:n_act] if chunk_dim is None else scratch[n_w:n_w + n_act]

    def round_weights():
        for w_ref, buf in zip(weights, bufs):
            buf[...] = w_ref[...].astype(BF16)

    def run_rows(rows):
        out = compute([a[rows, :] for a in acts], [b[...] for b in bufs],
                      [e[rows, :] for e in extras])
        o_ref[rows, :] = out.astype(o_ref.dtype)

    if chunk_dim is None:
        round_weights()
        run_rows(slice(None))
        return

    acts_hbm = refs[:n_act]
    sem = scratch[n_w + n_act]
    i, j = pl.program_id(0), pl.program_id(1)
    ni, nj = pl.num_programs(0), pl.num_programs(1)
    cs = (tm if chunk_dim == 0 else acts[0].shape[1]) // ACT_CHUNKS
    chunk = lambda c: slice(c * cs, (c + 1) * cs)

    def act_copy(block, a, c):
        if chunk_dim == 0:
            src = acts_hbm[a].at[pl.ds(pl.multiple_of(block * tm + c * cs, cs), cs), :]
            dst = acts[a].at[chunk(c), :]
        else:
            src = acts_hbm[a].at[pl.ds(pl.multiple_of(block * tm, tm), tm), chunk(c)]
            dst = acts[a].at[:, chunk(c)]
        return pltpu.make_async_copy(src, dst, sem.at[a, c])

    def step(before=None, after=None):
        round_weights()
        total = None
        for c in range(ACT_CHUNKS):
            if before is not None:
                before(c)
            if chunk_dim == 0:
                run_rows(chunk(c))
            else:
                part = compute([a[:, chunk(c)] for a in acts], [b[chunk(c), :] for b in bufs],
                               [e[...] for e in extras])
                total = part if total is None else total + part
            if after is not None:
                after(c)
        if chunk_dim == 1:
            o_ref[...] = total.astype(o_ref.dtype)

    def wait_chunk(c):
        for a in range(n_act):
            act_copy(i, a, c).wait()

    def start_next_chunk(c):
        @pl.when(i + 1 < ni)
        def _():
            for a in range(n_act):
                act_copy(i + 1, a, c).start()

    @pl.when(j == 0)
    def _():
        @pl.when(i == 0)
        def _():
            for a in range(n_act):
                for c in range(ACT_CHUNKS):
                    act_copy(0, a, c).start()

        step(before=wait_chunk)

    @pl.when((j > 0) & (j < nj - 1))
    def _():
        round_weights()
        run_rows(slice(None))

    @pl.when((j > 0) & (j == nj - 1))
    def _():
        step(after=start_next_chunk)


def _fused_call(compute, acts, weights, col_ofs, extras, extra_col_ofs, *, layer, tm, tn, nj,
                out_cols, out_dtype, name, chunk_dim=0):
    assert nj >= 2 or chunk_dim is None
    m = acts[0].shape[0]
    k = weights[0].shape[-2]
    scratch = [pltpu.VMEM((k, tn), BF16) for _ in weights]
    if chunk_dim is not None:
        in_specs = [pl.BlockSpec(memory_space=pl.ANY) for _ in acts]
        scratch += [pltpu.VMEM((tm, a.shape[1]), a.dtype) for a in acts]
        scratch += [pltpu.SemaphoreType.DMA((len(acts), ACT_CHUNKS))]
    else:
        in_specs = [_resident((tm, a.shape[1]), lambda i, j: (i, 0)) for a in acts]
    in_specs += [_w_spec(w, k, tn, layer, c) for w, c in zip(weights, col_ofs)]
    in_specs += [pl.BlockSpec((tm, tn), lambda i, j, c=c: (i, c(j))) for c in extra_col_ofs]
    return pl.pallas_call(
        functools.partial(_fused_kernel, n_act=len(acts), n_w=len(weights),
                          n_extra=len(extras), tm=tm, compute=compute, chunk_dim=chunk_dim),
        grid=(m // tm, nj),
        in_specs=in_specs,
        out_specs=pl.BlockSpec((tm, tn), lambda i, j: (i, j)),
        out_shape=jax.ShapeDtypeStruct((m, out_cols), out_dtype),
        scratch_shapes=scratch,
        compiler_params=_cparams(2),
        name=name,
    )(*acts, *weights, *extras)


def matmul(a, w, *, tm, tn, layer, chunk_dim, out_dtype=F32, name="matmul"):
    n = w.shape[-1]
    return _fused_call(lambda acts, ws, extras: _dot(acts[0], ws[0]), [a], [w], [lambda j: j],
                       [], [], layer=layer, tm=tm, tn=tn, nj=n // tn, out_cols=n,
                       out_dtype=out_dtype, name=name, chunk_dim=chunk_dim)


def _sigmoid(x):
    return 1.0 / (1.0 + jnp.exp(-x))


def _silu(x):
    return x * _sigmoid(x)


def _ffn_in_compute(acts, ws, extras):
    return _silu(_dot(acts[0], ws[0])) * _dot(acts[0], ws[1])


def ffn_in(xn, w, layer, *, tm=4096, tn=256):
    nj = D_FF // tn
    return _fused_call(_ffn_in_compute, [xn], [w, w], [lambda j: j, lambda j: nj + j], [], [],
                       layer=layer, tm=tm, tn=tn, nj=nj, out_cols=D_FF, out_dtype=BF16,
                       name="ffn_in", chunk_dim=None)


MERGE_ROWS = 512


def _merge_compute(acts, ws, gates):
    outs = []
    for r in range(0, acts[0].shape[0], MERGE_ROWS):
        rows = slice(r, r + MERGE_ROWS)
        outs.append(_sigmoid(gates[0][rows]) * _dot(acts[0][rows], ws[0])
                    + _sigmoid(gates[1][rows]) * _dot(acts[1][rows], ws[1]))
    return jnp.concatenate(outs, axis=0)


def merge_branches(oa, ob, wa, wb, layer, rest, *, tm=2048, tn=256):
    n = wa.shape[-1]
    ga0, gb0 = OFF_GA // tn, OFF_GB // tn
    return _fused_call(_merge_compute, [oa, ob], [wa, wb], [lambda j: j, lambda j: j],
                       [rest, rest], [lambda j: ga0 + j, lambda j: gb0 + j],
                       layer=layer, tm=tm, tn=tn, nj=n // tn, out_cols=n, out_dtype=BF16,
                       name="merge_branches")


def _dot_nt(a, b):
    return lax.dot_general(a, b, (((1,), (1,)), ((), ())), preferred_element_type=F32)


def _matmul_nt_kernel(a_ref, wt_ref, *rest):
    o_ref, wb_ref = rest[-2:]
    wb_ref[...] = wt_ref[0].astype(BF16)
    o_ref[...] = _dot_nt(a_ref[...], wb_ref[...])
    if len(rest) == 4:
        side_w_ref, side_o_ref = rest[:2]

        @pl.when(pl.program_id(1) == 0)
        def _():
            side_o_ref[...] = _dot_nt(a_ref[...], side_w_ref[0].astype(BF16))


def matmul_nt(a, wt, *, layer, row0, n, tm, tn, name, side=None):
    m, k = a.shape
    assert row0 % SUBLANES == 0
    w_rows = lambda rows, start: pl.BlockSpec(
        (pl.Element(1), pl.Element(rows), pl.Element(k)),
        lambda i, j: (layer, pl.multiple_of(start(j), SUBLANES), 0))
    in_specs = [_resident((tm, k), lambda i, j: (i, 0)), w_rows(tn, lambda j: row0 + j * tn)]
    out_specs = [pl.BlockSpec((tm, tn), lambda i, j: (i, j))]
    out_shape = [jax.ShapeDtypeStruct((m, n), F32)]
    operands = [a, wt]
    if side is not None:
        side_row0, side_n = side
        assert side_row0 % SUBLANES == 0
        in_specs.insert(2, w_rows(side_n, lambda j: side_row0 + 0 * j))
        out_specs.insert(0, pl.BlockSpec((tm, side_n), lambda i, j: (i, 0)))
        out_shape.insert(0, jax.ShapeDtypeStruct((m, side_n), F32))
        operands.append(wt)
    outs = pl.pallas_call(
        _matmul_nt_kernel,
        grid=(m // tm, n // tn),
        in_specs=in_specs,
        out_specs=out_specs,
        out_shape=out_shape,
        scratch_shapes=[pltpu.VMEM((tn, k), BF16)],
        compiler_params=_cparams(2),
        name=name,
    )(*operands)
    return outs[0] if side is None else (outs[1], outs[0])


def _shift_rows(x, s):
    r = pltpu.roll(x, s, axis=0)
    row = lax.broadcasted_iota(jnp.int32, (SUBLANES, x.shape[1]), 0)
    head = jnp.where(row >= s, r[0:SUBLANES], 0.0)
    return jnp.concatenate([head, r[SUBLANES:]], axis=0)


def _causal_conv(x, w, width):
    acc = _shift_rows(x, width - 1) * w[0:1, :]
    for j in range(1, width):
        s = width - 1 - j
        term = x if s == 0 else _shift_rows(x, s)
        acc = acc + term * w[j:j + 1, :]
    return acc


def _causal_conv_tile(x_ref, cols, w, r0, n_rows, width):
    halo0 = pl.multiple_of(jnp.maximum(r0 - SUBLANES, 0), SUBLANES)
    halo = jnp.where(r0 > 0, x_ref[pl.ds(halo0, SUBLANES), cols], 0.0)
    ext = jnp.concatenate([halo, x_ref[pl.ds(r0, n_rows), cols]], axis=0)
    acc = None
    for j in range(width):
        s = width - 1 - j
        term = ext[SUBLANES - s:SUBLANES - s + n_rows] * w[j:j + 1, :]
        acc = term if acc is None else acc + term
    return acc


def _short_conv_kernel(b_ref, c_ref, x_ref, w_ref, o_ref):
    acc = _causal_conv(c_ref[...] * x_ref[...], w_ref[...], SC_CONV)
    o_ref[...] = (b_ref[...] * acc).astype(o_ref.dtype)


def short_conv(rest, w_t, *, tc=256):
    nb = D_SC // tc
    return pl.pallas_call(
        _short_conv_kernel,
        grid=(BATCH, nb),
        in_specs=[pl.BlockSpec((SEQ, tc), lambda b, j: (b, OFF_SCB // tc + j)),
                  pl.BlockSpec((SEQ, tc), lambda b, j: (b, OFF_SCC // tc + j)),
                  pl.BlockSpec((SEQ, tc), lambda b, j: (b, OFF_SCX // tc + j)),
                  pl.BlockSpec((SC_CONV, tc), lambda b, j: (0, j))],
        out_specs=pl.BlockSpec((SEQ, tc), lambda b, j: (b, j)),
        out_shape=jax.ShapeDtypeStruct((M_TOK, D_SC), BF16),
        compiler_params=_cparams(2),
        name="short_conv",
    )(rest, rest, rest, w_t)


def _l2norm(x):
    return x * lax.rsqrt(jnp.sum(x * x, axis=-1, keepdims=True) + EPS)


def _pick_lane(x, lane, idx):
    picked = jnp.sum(jnp.where(lane == idx, x, 0.0), axis=-1, keepdims=True)
    return jnp.broadcast_to(picked, x.shape)


def _deltanet_kernel(xq_ref, xk_ref, xv_ref, z_ref, ab_ref, wq_ref, wk_ref, wv_ref,
                     alog_ref, dtb_ref, ng_ref, o_ref,
                     gcall_s, betaall_s, kq_s, u_s, attn_s, m_s, b_s, egl_s, hist_s):
    nb = SB_PER_ITER
    hps = HEADS_PER_STEP
    head0 = pl.program_id(1) * hps

    ab = ab_ref[...]
    g_all = -jnp.exp(alog_ref[...]) * jax.nn.softplus(ab + dtb_ref[...])
    betaall_s[...] = _sigmoid(ab)
    r_i = lax.broadcasted_iota(jnp.int32, (SUPER, SUPER), 0)
    c_i = lax.broadcasted_iota(jnp.int32, (SUPER, SUPER), 1)
    same = (r_i // CHUNK) == (c_i // CHUNK)
    incl = same & (r_i >= c_i)
    strict = same & (r_i > c_i)
    eye = (r_i == c_i).astype(F32)
    first_half = r_i < CHUNK

    tri = incl.astype(BF16)
    tiles = [g_all[t * SUPER:(t + 1) * SUPER] for t in range(N_SUPER)]
    hi = [g.astype(BF16) for g in tiles]
    r1 = [g - h.astype(F32) for g, h in zip(tiles, hi)]
    mid = [r.astype(BF16) for r in r1]
    lo = [(r - m.astype(F32)).astype(BF16) for r, m in zip(r1, mid)]
    for t in range(N_SUPER):
        gcall_s[t * SUPER:(t + 1) * SUPER, :] = (
            (_dot(tri, hi[t]) + _dot(tri, mid[t])) + _dot(tri, lo[t]))

    def phase1(it, carry, hd):
        cols = slice(hd * HEAD_D, (hd + 1) * HEAD_D)
        sbs = [it * nb + s for s in range(nb)]
        r0s = [pl.multiple_of(sb * SUPER, SUPER) for sb in sbs]
        rows = [pl.ds(r0, SUPER) for r0 in r0s]

        def conv_silu(x_ref, w_ref, r0):
            return _silu(_causal_conv_tile(x_ref, cols, w_ref[:, cols], r0, SUPER, DN_CONV))

        q = [_l2norm(conv_silu(xq_ref, wq_ref, r0)) * (HEAD_D ** -0.5) for r0 in r0s]
        k = [_l2norm(conv_silu(xk_ref, wk_ref, r0)) for r0 in r0s]
        v = [conv_silu(xv_ref, wv_ref, r0) for r0 in r0s]
        cb = [_pick_lane(gcall_s[r, :], c_i, head0 + hd) for r in rows]
        beta = [_pick_lane(betaall_s[r, :], c_i, N_HEADS + head0 + hd) for r in rows]
        kb = [k[s] * beta[s] for s in range(nb)]
        kk = [_bdot_nt(kb[s], k[s]) for s in range(nb)]
        qk = [_bdot_nt(q[s], k[s]) for s in range(nb)]
        decay = [jnp.exp(jnp.where(incl, cb[s] - cb[s].T, -jnp.inf)) for s in range(nb)]
        p = [-jnp.where(strict, kk[s] * decay[s], 0.0) for s in range(nb)]
        t = [eye + p[s] for s in range(nb)]
        for s in range(nb):
            attn_s[hd, rows[s], :] = (qk[s] * decay[s]).astype(BF16)
        n_sq = 1
        while 2 * n_sq < CHUNK:
            p = [_bdot(p[s], p[s]) for s in range(nb)]
            t = [t[s] + _bdot(t[s], p[s]) for s in range(nb)]
            n_sq *= 2
        egc = [jnp.exp(cb[s]) for s in range(nb)]
        rhs = [jnp.concatenate([v[s] * beta[s], kb[s] * egc[s]], axis=1) for s in range(nb)]
        uk = [_bdot(t[s], rhs[s]) for s in range(nb)]
        gl = [jnp.where(first_half,
                        jnp.broadcast_to(cb[s][CHUNK - 1:CHUNK, :], (SUPER, SUPER)),
                        jnp.broadcast_to(cb[s][SUPER - 1:SUPER, :], (SUPER, SUPER)))
              for s in range(nb)]
        kd = [k[s] * jnp.exp(gl[s] - cb[s]) for s in range(nb)]
        rhs2 = [jnp.concatenate([uk[s][:, HEAD_D:], uk[s][:, :HEAD_D]], axis=1)
                for s in range(nb)]
        mb0 = [_bdot_tn(jnp.where(first_half, kd[s], 0.0), rhs2[s]) for s in range(nb)]
        mb1 = [_bdot_tn(jnp.where(first_half, 0.0, kd[s]), rhs2[s]) for s in range(nb)]
        for s in range(nb):
            u_s[hd, rows[s], :] = uk[s][:, :HEAD_D]
            kcum = uk[s][:, HEAD_D:].astype(BF16)
            qd = (q[s] * egc[s]).astype(BF16)
            egl = jnp.exp(gl[s])
            for half, mb in enumerate((mb0[s], mb1[s])):
                c0 = pl.multiple_of((2 * sbs[s] + half) * SUPER, SUPER)
                lo = half * CHUNK
                kq_s[hd, pl.ds(c0, CHUNK), :] = kcum[lo:lo + CHUNK]
                kq_s[hd, pl.ds(c0 + CHUNK, CHUNK), :] = qd[lo:lo + CHUNK]
                m_s[hd, pl.ds(c0, SUPER), :] = mb[:, :HEAD_D].astype(BF16)
                b_s[hd, pl.ds(c0, SUPER), :] = mb[:, HEAD_D:]
                e0 = pl.multiple_of((2 * sbs[s] + half) * SUBLANES, SUBLANES)
                egl_s[hd, pl.ds(e0, SUBLANES), :] = egl[lo:lo + SUBLANES]
        return carry

    for hd in range(hps):
        lax.fori_loop(0, N_SUPER // nb, functools.partial(phase1, hd=hd), 0)

    def phase2(c, states):
        r = pl.ds(pl.multiple_of(c * SUPER, SUPER), SUPER)
        e = pl.ds(pl.multiple_of(c * SUBLANES, SUBLANES), SUBLANES)
        sb16 = [st.astype(BF16) for st in states]
        ms = [jnp.dot(m_s[hd, r, :], sb16[hd], preferred_element_type=F32) for hd in range(hps)]
        new = []
        for hd in range(hps):
            hist_s[hd, r, :] = sb16[hd]
            egl = egl_s[hd, e, :]
            scaled = (states[hd].reshape(HEAD_D // SUBLANES, SUBLANES, HEAD_D) * egl[None]
                      ).reshape(HEAD_D, HEAD_D)
            new.append(scaled + (b_s[hd, r, :] - ms[hd]))
        return tuple(new)

    lax.fori_loop(0, N_CHUNK, phase2,
                  tuple(jnp.zeros((HEAD_D, HEAD_D), F32) for _ in range(hps)))

    ng = ng_ref[...]

    def phase3(it, carry, hd):
        sbs = [it * nb + s for s in range(nb)]
        rows = [pl.ds(pl.multiple_of(sb * SUPER, SUPER), SUPER) for sb in sbs]
        ks = []
        for s in range(nb):
            for half in range(2):
                c = pl.ds(pl.multiple_of((2 * sbs[s] + half) * SUPER, SUPER), SUPER)
                ks.append(jnp.dot(kq_s[hd, c, :], hist_s[hd, c, :], preferred_element_type=F32))
        vnew, o1 = [], []
        for s in range(nb):
            u = u_s[hd, rows[s], :]
            k0, k1 = ks[2 * s], ks[2 * s + 1]
            vnew.append(jnp.concatenate([u[:CHUNK] - k0[:CHUNK], u[CHUNK:] - k1[:CHUNK]], axis=0))
            o1.append(jnp.concatenate([k0[CHUNK:], k1[CHUNK:]], axis=0))
        o = [o1[s] + jnp.dot(attn_s[hd, rows[s], :], vnew[s].astype(BF16),
                             preferred_element_type=F32) for s in range(nb)]
        for s in range(nb):
            z = z_ref[rows[s], hd * HEAD_D:(hd + 1) * HEAD_D]
            on = (o[s] * _rms_scale(o[s])) * ng
            o_ref[rows[s], hd * HEAD_D:(hd + 1) * HEAD_D] = (on * _silu(z)).astype(o_ref.dtype)
        return carry

    for hd in range(hps):
        lax.fori_loop(0, N_SUPER // nb, functools.partial(phase3, hd=hd), 0)


def deltanet(qkv, rest, ab, conv_w_t, a_log, dt_bias, norm_g):
    hps = HEADS_PER_STEP
    wblk = hps * HEAD_D
    seq_blk = lambda off: pl.BlockSpec((SEQ, wblk), lambda b, h: (b, off // wblk + h))
    w_blk = lambda off: pl.BlockSpec((DN_CONV, wblk), lambda b, h: (0, off // wblk + h))
    vec = pl.BlockSpec((1, HEAD_D), lambda b, h: (0, 0))
    pad = lambda x: jnp.pad(x, (0, HEAD_D - N_HEADS)).reshape(1, HEAD_D)
    f32_seq = pltpu.VMEM((SEQ, HEAD_D), F32)
    per_chunk = lambda dt: pltpu.VMEM((hps, N_CHUNK * SUPER, HEAD_D), dt)
    return pl.pallas_call(
        _deltanet_kernel,
        grid=(BATCH, N_HEADS // hps),
        in_specs=[seq_blk(0), seq_blk(D_DN), seq_blk(2 * D_DN),
                  pl.BlockSpec((SEQ, wblk), lambda b, h: (b, OFF_Z // wblk + h)),
                  pl.BlockSpec((SEQ, HEAD_D), lambda b, h: (b, 0)),
                  w_blk(0), w_blk(D_DN), w_blk(2 * D_DN), vec, vec, vec],
        out_specs=pl.BlockSpec((SEQ, wblk), lambda b, h: (b, h)),
        out_shape=jax.ShapeDtypeStruct((M_TOK, D_DN), BF16),
        scratch_shapes=[f32_seq, f32_seq,
                        per_chunk(BF16),
                        pltpu.VMEM((hps, SEQ, HEAD_D), F32),
                        pltpu.VMEM((hps, SEQ, HEAD_D), BF16),
                        per_chunk(BF16),
                        per_chunk(F32),
                        pltpu.VMEM((hps, N_CHUNK * SUBLANES, HEAD_D), F32),
                        per_chunk(BF16)],
        compiler_params=_cparams(2),
        name="deltanet",
    )(qkv, qkv, qkv, rest, ab, conv_w_t, conv_w_t, conv_w_t,
      pad(a_log), pad(dt_bias), norm_g.reshape(1, HEAD_D))


def _ffn_block(h, xn, w_in, w_out, layer, gpost, gpre_next):
    act = ffn_in(xn, w_in, layer)
    f = matmul(act, w_out, layer=layer, tm=1024, tn=256, chunk_dim=1, name="ffn_out")
    return resid_norm(h, f, gpost, gpre_next, 0.5)


def _mixer_block(h, xn, w_in, layer, dn_conv_w, dn_a_log, dn_dt_bias, dn_norm_g, w_dn_out,
                 sc_conv_w, w_sc_out, w_o, gpost, gpre_next):
    wt = jnp.swapaxes(w_in, 1, 2)
    proj = functools.partial(matmul_nt, xn, wt, layer=layer, tm=2048)
    qkv, ab = proj(row0=0, n=D_QKV, tn=512, name="in_proj_qkv", side=(D_QKV, HEAD_D))
    rest = proj(row0=D_QKV + 2 * N_HEADS, n=D_REST, tn=512, name="in_proj_rest")
    o_dn = deltanet(qkv, rest, ab, dn_conv_w.T, dn_a_log, dn_dt_bias, dn_norm_g)
    o_sc = short_conv(rest, sc_conv_w.T)
    merged = merge_branches(o_dn, o_sc, w_dn_out, w_sc_out, layer, rest)
    m = matmul(merged, w_o, layer=layer, tm=2048, tn=512, chunk_dim=0, name="out_proj")
    return resid_norm(h, m, gpost, gpre_next, 1.0)


def kernel(x, ffn1_pre_g, ffn1_post_g, w_ffn1_in, w_ffn1_out, mix_pre_g, mix_post_g, w_in,
           dn_conv_w, dn_a_log, dn_dt_bias, dn_norm_g, w_dn_out, sc_conv_w, w_sc_out, w_o,
           ffn2_pre_g, ffn2_post_g, w_ffn2_in, w_ffn2_out):
    h = x.reshape(M_TOK, D_MODEL)
    xn = rmsnorm_cast(h, ffn1_pre_g[0])
    for l in range(DEPTH):
        h, xn = _ffn_block(h, xn, w_ffn1_in, w_ffn1_out, l, ffn1_post_g[l], mix_pre_g[l])
        h, xn = _mixer_block(h, xn, w_in, l, dn_conv_w[l], dn_a_log[l], dn_dt_bias[l],
                             dn_norm_g[l], w_dn_out, sc_conv_w[l], w_sc_out, w_o,
                             mix_post_g[l], ffn2_pre_g[l])
        nxt = ffn1_pre_g[l + 1] if l + 1 < DEPTH else None
        h, xn = _ffn_block(h, xn, w_ffn2_in, w_ffn2_out, l, ffn2_post_g[l], nxt)
    return h.reshape(BATCH, SEQ, D_MODEL)
```
